```python
import math
import jax, jax.numpy as jnp
from jax import lax
import numpy as np

D_MODEL = 1024
BATCH = 32
SEQ = 2048
DEPTH = 2

HEAD_DIM = 64
A_HEADS = 8
IDX_HEADS = 4
IDX_DIM = 64
TOPK_MAX = 256
B_HEADS = 4
B_VDIM = 2 * HEAD_DIM
D_A = A_HEADS * HEAD_DIM
D_B = B_HEADS * B_VDIM
FFN_HIDDEN = -(-8 * D_MODEL // (3 * 256)) * 256
ROPE_THETA = 10000.0
EPS = 1e-6
DENSE_Q_BLOCK = 128
SPARSE_Q_BLOCK = 32
POS_OFFSET_MAX = 4096
IN_WIDTHS = (D_A, D_A, D_A,
             IDX_HEADS * IDX_DIM, IDX_DIM, IDX_HEADS,
             B_HEADS * 2 * HEAD_DIM, B_HEADS * 2 * HEAD_DIM, D_B,
             2 * D_MODEL)
D_IN = sum(IN_WIDTHS)

kernel_name = "hybrid_dsa_diffattn_gated_swiglu"


def rms_norm(x, g):
    x32 = x.astype(jnp.float32)
    y = x32 * lax.rsqrt(jnp.mean(x32 * x32, axis=-1, keepdims=True) + EPS)
    return (y * g.astype(jnp.float32)).astype(x.dtype)


def rope_tables(positions):
    inv_freq = 1.0 / (ROPE_THETA ** (jnp.arange(0, HEAD_DIM, 2, dtype=jnp.float32) / HEAD_DIM))
    ang = positions.astype(jnp.float32)[..., None] * inv_freq
    return jnp.cos(ang)[:, :, None, :], jnp.sin(ang)[:, :, None, :]


def apply_rope(x, cos, sin):
    x32 = x.astype(jnp.float32)
    x1, x2 = jnp.split(x32, 2, axis=-1)
    out = jnp.concatenate([x1 * cos - x2 * sin, x2 * cos + x1 * sin], axis=-1)
    return out.astype(x.dtype)


def split_cols(proj):
    offs = np.cumsum(np.array(IN_WIDTHS))[:-1].tolist()
    return jnp.split(proj, offs, axis=-1)


def sparse_attention(q, k, v, q_idx, k_idx, w_idx, k_sel):
    B, S, H, dh = q.shape
    n_blk = S // SPARSE_Q_BLOCK
    key_pos = jnp.arange(S)
    k_idx32 = k_idx.astype(jnp.float32)

    def block(i):
        start = i * SPARSE_Q_BLOCK
        qb = lax.dynamic_slice_in_dim(q, start, SPARSE_Q_BLOCK, axis=1)
        qib = lax.dynamic_slice_in_dim(q_idx, start, SPARSE_Q_BLOCK, axis=1)
        wb = lax.dynamic_slice_in_dim(w_idx, start, SPARSE_Q_BLOCK, axis=1)
        q_pos = start + jnp.arange(SPARSE_Q_BLOCK)
        causal = key_pos[None, :] <= q_pos[:, None]
        dots = jnp.einsum('bqhd,bsd->bqhs', qib.astype(jnp.float32), k_idx32) * (IDX_DIM ** -0.5)
        score = jnp.einsum('bqh,bqhs->bqs', wb.astype(jnp.float32), jax.nn.relu(dots))
        score = jnp.where(causal[None], score, -jnp.inf)
        _, top_idx = lax.top_k(score, k_sel)
        valid = top_idx <= q_pos[None, :, None]
        kg = jax.vmap(lambda kb, ib: kb[ib])(k, top_idx)
        vg = jax.vmap(lambda vb, ib: vb[ib])(v, top_idx)
        logits = jnp.einsum('bqhd,bqkhd->bqhk', qb.astype(jnp.float32),
                            kg.astype(jnp.float32)) * (dh ** -0.5)
        logits = jnp.where(valid[:, :, None, :], logits, -jnp.inf)
        p = jax.nn.softmax(logits, axis=-1).astype(v.dtype)
        return jnp.einsum('bqhk,bqkhd->bqhd', p, vg)

    out = lax.map(block, jnp.arange(n_blk))
    return jnp.moveaxis(out, 0, 1).reshape(B, S, H, dh)


def diff_attention(q, k, v, lam):
    B, S, H, _, dh = q.shape
    n_blk = S // DENSE_Q_BLOCK
    key_pos = jnp.arange(S)
    k32 = k.astype(jnp.float32)

    def block(i):
        start = i * DENSE_Q_BLOCK
        qb = lax.dynamic_slice_in_dim(q, start, DENSE_Q_BLOCK, axis=1)
        q_pos = start + jnp.arange(DENSE_Q_BLOCK)
        causal = key_pos[None, :] <= q_pos[:, None]
        logits = jnp.einsum('bqhcd,bshcd->bhcqs', qb.astype(jnp.float32), k32) * (dh ** -0.5)
        logits = jnp.where(causal[None, None, None], logits, -jnp.inf)
        p = jax.nn.softmax(logits, axis=-1)
        attn = (p[:, :, 0] - lam * p[:, :, 1]).astype(v.dtype)
        return jnp.einsum('bhqs,bshe->bqhe', attn, v)

    out = lax.map(block, jnp.arange(n_blk))
    return jnp.moveaxis(out, 0, 1).reshape(B, S, H, 2 * dh)


def hybrid_layer(x, cos, sin, layer_idx, attn_norm, w_in, gate_bias, a_q_norm, a_k_norm,
                 idx_k_norm, b_q_norm, b_k_norm, diff_lambda, b_subln, w_up_a, w_up_b,
                 w_out, ffn_norm, w_ffn_in, w_ffn_out):
    B, S, _ = x.shape
    k_sel = min(TOPK_MAX, S // 4)
    h = rms_norm(x, attn_norm)
    proj = h @ w_in
    aq, ak, av, iq, ik, iw, bq, bk, bv, gates = split_cols(proj)

    aq = apply_rope(rms_norm(aq.reshape(B, S, A_HEADS, HEAD_DIM), a_q_norm), cos, sin)
    ak = apply_rope(rms_norm(ak.reshape(B, S, A_HEADS, HEAD_DIM), a_k_norm), cos, sin)
    av = av.reshape(B, S, A_HEADS, HEAD_DIM)
    iq = apply_rope(iq.reshape(B, S, IDX_HEADS, IDX_DIM), cos, sin)
    ik = apply_rope(rms_norm(ik, idx_k_norm)[:, :, None, :], cos, sin)[:, :, 0, :]
    iw = iw * (IDX_HEADS ** -0.5)
    ya = sparse_attention(aq, ak, av, iq, ik, iw, k_sel).reshape(B, S, D_A)

    bq = apply_rope(rms_norm(bq.reshape(B, S, 2 * B_HEADS, HEAD_DIM), b_q_norm), cos, sin)
    bk = apply_rope(rms_norm(bk.reshape(B, S, 2 * B_HEADS, HEAD_DIM), b_k_norm), cos, sin)
    bq = bq.reshape(B, S, B_HEADS, 2, HEAD_DIM)
    bk = bk.reshape(B, S, B_HEADS, 2, HEAD_DIM)
    bv = bv.reshape(B, S, B_HEADS, B_VDIM)
    lam_init = 0.8 - 0.6 * math.exp(-0.3 * layer_idx)
    dl = diff_lambda.astype(jnp.float32)
    lam = jnp.exp(jnp.sum(dl[0] * dl[1])) - jnp.exp(jnp.sum(dl[2] * dl[3])) + lam_init
    yb = diff_attention(bq, bk, bv, lam)
    yb = (rms_norm(yb, b_subln) * (1.0 - lam_init)).reshape(B, S, D_B)

    g = jax.nn.sigmoid((gates + gate_bias).astype(jnp.float32)).astype(x.dtype)
    g_a, g_b = jnp.split(g, 2, axis=-1)
    merged = g_a * (ya @ w_up_a) + g_b * (yb @ w_up_b)
    x = x + merged @ w_out

    h = rms_norm(x, ffn_norm)
    gate, up = jnp.split(h @ w_ffn_in, 2, axis=-1)
    return x + (jax.nn.silu(gate) * up) @ w_ffn_out


def setup_inputs(seed: int = 0) -> dict:
    key = jax.random.key(seed)
    ks = jax.random.split(key, 20)
    f32 = jnp.float32

    def dense(k, shape, fan_in):
        return jax.random.normal(k, shape, f32) * (fan_in ** -0.5)

    def gain(k, shape):
        return 1.0 + 0.02 * jax.random.normal(k, shape, f32)

    x = jax.random.normal(ks[0], (BATCH, SEQ, D_MODEL), f32)
    offset = jax.random.randint(ks[1], (BATCH, 1), 0, POS_OFFSET_MAX, dtype=jnp.int32)
    positions = (offset + jnp.arange(SEQ, dtype=jnp.int32)[None, :]).astype(jnp.int32)
    return {
        "x": x,
        "positions": positions,
        "attn_norm": gain(ks[2], (DEPTH, D_MODEL)),
        "w_in": dense(ks[3], (DEPTH, D_MODEL, D_IN), D_MODEL),
        "gate_bias": 0.1 * jax.random.normal(ks[4], (DEPTH, 2 * D_MODEL), f32),
        "a_q_norm": gain(ks[5], (DEPTH, HEAD_DIM)),
        "a_k_norm": gain(ks[6], (DEPTH, HEAD_DIM)),
        "idx_k_norm": gain(ks[7], (DEPTH, IDX_DIM)),
        "b_q_norm": gain(ks[8], (DEPTH, HEAD_DIM)),
        "b_k_norm": gain(ks[9], (DEPTH, HEAD_DIM)),
        "diff_lambda": 0.1 * jax.random.normal(ks[10], (DEPTH, 4, HEAD_DIM), f32),
        "b_subln": gain(ks[11], (DEPTH, B_VDIM)),
        "w_up_a": dense(ks[12], (DEPTH, D_A, D_MODEL), D_A),
        "w_up_b": dense(ks[13], (DEPTH, D_B, D_MODEL), D_B),
        "w_out": dense(ks[14], (DEPTH, D_MODEL, D_MODEL), D_MODEL),
        "ffn_norm": gain(ks[15], (DEPTH, D_MODEL)),
        "w_ffn_in": dense(ks[16], (DEPTH, D_MODEL, 2 * FFN_HIDDEN), D_MODEL),
        "w_ffn_out": dense(ks[17], (DEPTH, FFN_HIDDEN, D_MODEL), FFN_HIDDEN),
    }


def reference(x, positions, attn_norm, w_in, gate_bias, a_q_norm, a_k_norm, idx_k_norm,
              b_q_norm, b_k_norm, diff_lambda, b_subln, w_up_a, w_up_b, w_out,
              ffn_norm, w_ffn_in, w_ffn_out):
    cos, sin = rope_tables(positions)
    for l in range(DEPTH):
        x = hybrid_layer(x, cos, sin, l, attn_norm[l], w_in[l], gate_bias[l], a_q_norm[l],
                         a_k_norm[l], idx_k_norm[l], b_q_norm[l], b_k_norm[l],
                         diff_lambda[l], b_subln[l], w_up_a[l], w_up_b[l], w_out[l],
                         ffn_norm[l], w_ffn_in[l], w_ffn_out[l])
    return x
```

```python
import functools
import math

import jax
import jax.numpy as jnp
from jax import lax
from jax.experimental import pallas as pl
from jax.experimental.pallas import tpu as pltpu

F32 = jnp.float32
BF16 = jnp.bfloat16
I32 = jnp.int32

HEAD_DIM = 64
HALF = HEAD_DIM // 2
A_HEADS = 8
IDX_HEADS = 4
B_HEADS = 4
TOPK_MAX = 256
ROPE_THETA = 10000.0
EPS = 1e-6
LANES = 128
MASKED = -1e30
INT_MIN = -(2 ** 31)
VMEM_LIMIT = 56 * 1024 * 1024

D_A = A_HEADS * HEAD_DIM
D_B = B_HEADS * 2 * HEAD_DIM
IDX_W = IDX_HEADS * HEAD_DIM

SEG_AQ = 0
SEG_AK = SEG_AQ + D_A
SEG_AV = SEG_AK + D_A
SEG_IQ = SEG_AV + D_A
SEG_IK = SEG_IQ + IDX_W
SEG_IW = SEG_IK + LANES
SEG_BQ = SEG_IW + LANES
SEG_BK = SEG_BQ + D_B
SEG_BV = SEG_BK + D_B
SEG_G = SEG_BV + D_B


def _const_spec(shape):
    nd = len(shape)
    return pl.BlockSpec(shape, lambda *_: (0,) * nd, pipeline_mode=pl.Buffered(1))


def _params(*sem):
    return pltpu.CompilerParams(dimension_semantics=sem, vmem_limit_bytes=VMEM_LIMIT)


def _trig_kernel(pos_ref, invf_ref, cos_ref, sin_ref):
    ang = pos_ref[...].astype(F32) * invf_ref[...]
    cos_ref[...] = jnp.cos(ang)
    sin_ref[...] = jnp.sin(ang)


def _rope_tables(positions):
    n = positions.size
    per_row = LANES // HALF
    rows = n // per_row
    inv_freq = 1.0 / (ROPE_THETA ** (jnp.arange(0, HEAD_DIM, 2, dtype=F32) / HEAD_DIM))
    invf = jnp.tile(inv_freq, per_row)[None, :]
    pos = jnp.repeat(positions.reshape(rows, per_row), HALF, axis=1)
    tr = min(rows, 2048)
    cos, sin = pl.pallas_call(
        _trig_kernel,
        grid=(rows // tr,),
        in_specs=[pl.BlockSpec((tr, LANES), lambda i: (i, 0)), _const_spec((1, LANES))],
        out_specs=[pl.BlockSpec((tr, LANES), lambda i: (i, 0))] * 2,
        out_shape=[jax.ShapeDtypeStruct((rows, LANES), F32)] * 2,
        compiler_params=_params("parallel"),
        name="rope_trig",
    )(pos, invf)
    cos = cos.reshape(n, HALF)
    sin = sin.reshape(n, HALF)
    cos = jnp.concatenate([cos, cos, cos, cos], axis=1)
    sin = jnp.concatenate([-sin, sin, -sin, sin], axis=1)
    return cos, sin


def _rope(y, cos, sin, first_half):
    cols = []
    for c in range(y.shape[1] // LANES):
        t = y[:, c * LANES:(c + 1) * LANES]
        partner = jnp.where(first_half, pltpu.roll(t, LANES - HALF, 1), pltpu.roll(t, HALF, 1))
        cols.append(t * cos + partner * sin)
    return cols[0] if len(cols) == 1 else jnp.concatenate(cols, axis=1)


def _head_sumsq(y, bd):
    sq = (y * y).astype(BF16)
    w = min(bd.shape[0], y.shape[1])
    bd = bd[:w, :w]
    cols = [jnp.dot(sq[:, c * w:(c + 1) * w], bd, preferred_element_type=F32)
            for c in range(y.shape[1] // w)]
    return cols[0] if len(cols) == 1 else jnp.concatenate(cols, axis=1)


def _inproj_kernel(x_ref, gn_ref, w_ref, bd_ref, gaq_ref, gak_ref, gik_ref, gbq_ref, gbk_ref,
                   cos_ref, sin_ref,
                   aq_ref, ak_ref, av_ref, iq_ref, ik_ref, iw_ref, bq_ref, bk_ref, bv_ref, g_ref):
    x = x_ref[...]
    ms = jnp.mean(x * x, axis=-1, keepdims=True)
    h = (x * lax.rsqrt(ms + EPS) * gn_ref[...]).astype(BF16)
    cos = cos_ref[...]
    sin = sin_ref[...]
    bd = bd_ref[...]
    lane = lax.broadcasted_iota(I32, (1, LANES), 1)
    first_half = (lane % HEAD_DIM) < HALF
    scale = HEAD_DIM ** -0.5

    def proj(start, width):
        return jnp.dot(h, w_ref[:, start:start + width], preferred_element_type=F32)

    def normed_rope(start, width, gain_ref, out_scale):
        y = proj(start, width)
        y = y * lax.rsqrt(_head_sumsq(y, bd) * (1.0 / HEAD_DIM) + EPS) * gain_ref[...]
        y = _rope(y, cos, sin, first_half)
        return y if out_scale is None else y * out_scale

    aq_ref[...] = normed_rope(SEG_AQ, D_A, gaq_ref, scale).astype(BF16)
    ak_ref[...] = normed_rope(SEG_AK, D_A, gak_ref, None).astype(BF16)
    av_ref[...] = proj(SEG_AV, D_A).astype(BF16)
    iq_ref[...] = _rope(proj(SEG_IQ, IDX_W), cos, sin, first_half) * scale
    ik_ref[...] = normed_rope(SEG_IK, LANES, gik_ref, None)
    iw_ref[...] = proj(SEG_IW, LANES) * (IDX_HEADS ** -0.5)
    bq_ref[...] = normed_rope(SEG_BQ, D_B, gbq_ref, scale).astype(BF16)
    bk_ref[...] = normed_rope(SEG_BK, D_B, gbk_ref, None).astype(BF16)
    bv_ref[...] = proj(SEG_BV, D_B).astype(BF16)
    g_ref[...] = proj(SEG_G, g_ref.shape[1]).astype(BF16)


def _pack_w_in(w_in, d_model):
    widths = (D_A, D_A, D_A, IDX_W, HEAD_DIM, IDX_HEADS, D_B, D_B, D_B, 2 * d_model)
    offs = [0]
    for wd in widths:
        offs.append(offs[-1] + wd)
    aq, ak, av, iq, ik, iw, bq, bk, bv, gates = [w_in[:, offs[i]:offs[i + 1]] for i in range(10)]
    iw = jnp.pad(iw, ((0, 0), (0, LANES - IDX_HEADS)))
    return jnp.concatenate([aq, ak, av, iq, ik, ik, iw, bq, bk, bv, gates], axis=1).astype(BF16)


def _in_projection(x2, cos, sin, attn_norm, w_in, a_q_norm, a_k_norm, idx_k_norm, b_q_norm,
                   b_k_norm, tm):
    n, d = x2.shape
    wp = _pack_w_in(w_in, d)
    dp = wp.shape[1]
    blk = 2 * LANES
    r = jnp.arange(blk)
    bd = (r[:, None] // HEAD_DIM == r[None, :] // HEAD_DIM).astype(BF16)

    def tiled(g, reps):
        return jnp.tile(g.astype(F32), reps)[None, :]

    row = lambda w: pl.BlockSpec((tm, w), lambda i: (i, 0))
    out_w = [(D_A, BF16), (D_A, BF16), (D_A, BF16), (IDX_W, F32), (LANES, F32), (LANES, F32),
             (D_B, BF16), (D_B, BF16), (D_B, BF16), (2 * d, BF16)]
    return pl.pallas_call(
        _inproj_kernel,
        grid=(n // tm,),
        in_specs=[row(d), _const_spec((1, d)), _const_spec((d, dp)), _const_spec((blk, blk)),
                  _const_spec((1, D_A)), _const_spec((1, D_A)), _const_spec((1, LANES)),
                  _const_spec((1, D_B)), _const_spec((1, D_B)), row(LANES), row(LANES)],
        out_specs=[row(w) for w, _ in out_w],
        out_shape=[jax.ShapeDtypeStruct((n, w), dt) for w, dt in out_w],
        compiler_params=_params("parallel"),
        name="in_projection",
    )(x2, attn_norm.astype(F32)[None, :], wp, bd,
      tiled(a_q_norm, A_HEADS), tiled(a_k_norm, A_HEADS), tiled(idx_k_norm, 2),
      tiled(b_q_norm, 2 * B_HEADS), tiled(b_k_norm, 2 * B_HEADS), cos, sin)


def _split_bf16(a):
    hi = a.astype(BF16)
    lo = (a - hi.astype(F32)).astype(BF16)
    return hi, lo


def _dot_t(a, b):
    return lax.dot_general(a, b, (((1,), (1,)), ((), ())), preferred_element_type=F32)


def _sparse_kernel(iq_ref, iw_ref, ik_ref, aq_ref, ak_ref, av_ref, o_ref,
                   key_scr, tie_scr, bias_scr, *, k_sel, tq):
    j = pl.program_id(1)
    n_chunks = j + 1
    lane = lax.broadcasted_iota(I32, (1, LANES), 1)
    low_half = lane < HEAD_DIM
    row_pos = j * tq + lax.broadcasted_iota(I32, (tq, 1), 0)
    col_iota = lax.broadcasted_iota(I32, (1, tq), 1)

    iw = iw_ref[...]
    w_cols = [jnp.sum(jnp.where(lane == hh, iw, 0.0), axis=1, keepdims=True)
              for hh in range(IDX_HEADS)]
    iq_parts = []
    for hh in range(IDX_HEADS):
        pair = iq_ref[:, (hh // 2) * LANES:(hh // 2 + 1) * LANES]
        own = low_half if hh % 2 == 0 else jnp.logical_not(low_half)
        iq_parts.append(_split_bf16(jnp.where(own, pair, 0.0)))

    def score_chunk(c, carry):
        start = pl.multiple_of(c * tq, tq)
        k_hi, k_lo = _split_bf16(ik_ref[pl.ds(start, tq), :])
        score = jnp.zeros((tq, tq), F32)
        for hh in range(IDX_HEADS):
            q_hi, q_lo = iq_parts[hh]
            d = _dot_t(q_hi, k_hi) + _dot_t(q_hi, k_lo) + _dot_t(q_lo, k_hi)
            score = score + w_cols[hh] * jnp.maximum(d, 0.0)
        score = jnp.where(score == 0.0, 0.0, score)
        score = jnp.where(c * tq + col_iota <= row_pos, score, -jnp.inf)
        bits = lax.bitcast_convert_type(score, I32)
        key_scr[c] = bits ^ (lax.shift_right_arithmetic(bits, 31) & jnp.int32(0x7FFFFFFF))
        return carry

    lax.fori_loop(0, n_chunks, score_chunk, 0)

    def count(scr, pred):
        def body(c, acc):
            return acc + jnp.sum(jnp.where(pred(scr[c]), 1.0, 0.0), axis=1, keepdims=True)
        return lax.fori_loop(0, n_chunks, body, jnp.zeros((tq, 1), F32))

    def key_bit(i, t_u):
        cand_u = t_u | lax.shift_left(jnp.int32(1), 31 - i)
        cand = cand_u ^ jnp.int32(INT_MIN)
        cnt = count(key_scr, lambda kk: kk >= cand)
        return jnp.where(cnt >= k_sel, cand_u, t_u)

    t_u = lax.fori_loop(0, 32, key_bit, jnp.zeros((tq, 1), I32))
    tau = t_u ^ jnp.int32(INT_MIN)
    need = k_sel - count(key_scr, lambda kk: kk > tau)

    n_keys = key_scr.shape[0] * tq

    def tie_chunk(c, carry):
        tie_scr[c] = jnp.where(key_scr[c] == tau, c * tq + col_iota, n_keys)
        return carry

    lax.fori_loop(0, n_chunks, tie_chunk, 0)
    idx_bits = max(1, (n_keys - 1).bit_length())

    def idx_bit(i, p):
        cand = p | lax.shift_left(jnp.int32(1), idx_bits - 1 - i)
        cnt = count(tie_scr, lambda tt: tt < cand)
        return jnp.where(cnt < need, cand, p)

    p_tie = lax.fori_loop(0, idx_bits, idx_bit, jnp.zeros((tq, 1), I32))

    def bias_chunk(c, carry):
        sel = (key_scr[c] > tau) | (tie_scr[c] <= p_tie)
        sel = sel & (c * tq + col_iota <= row_pos)
        bias_scr[c] = jnp.where(sel, 0.0, MASKED)
        return carry

    lax.fori_loop(0, n_chunks, bias_chunk, 0)

    pair_out = []
    for hh in range(A_HEADS):
        pr = hh // 2
        own = low_half if hh % 2 == 0 else jnp.logical_not(low_half)
        q = aq_ref[:, pr * LANES:(pr + 1) * LANES]
        q = jnp.where(own, q, jnp.zeros_like(q))

        def attend(c, carry, pr=pr, own=own, q=q):
            m, l, acc = carry
            start = pl.multiple_of(c * tq, tq)
            kc = ak_ref[pl.ds(start, tq), pr * LANES:(pr + 1) * LANES]
            vc = av_ref[pl.ds(start, tq), pr * LANES:(pr + 1) * LANES]
            vc = jnp.where(own, vc, jnp.zeros_like(vc))
            s = _dot_t(q, kc) + bias_scr[c]
            m_new = jnp.maximum(m, jnp.max(s, axis=1, keepdims=True))
            alpha = jnp.exp(m - m_new)
            p = jnp.exp(s - m_new)
            l = alpha * l + jnp.sum(p, axis=1, keepdims=True)
            acc = alpha * acc + jnp.dot(p.astype(BF16), vc, preferred_element_type=F32)
            return m_new, l, acc

        m0 = jnp.full((tq, 1), MASKED, F32)
        _, l, acc = lax.fori_loop(0, n_chunks, attend,
                                  (m0, jnp.zeros((tq, 1), F32), jnp.zeros((tq, LANES), F32)))
        out = acc / l
        if hh % 2 == 0:
            pair_out.append(out)
        else:
            pair_out[pr] = pair_out[pr] + out
    for pr in range(A_HEADS // 2):
        o_ref[:, pr * LANES:(pr + 1) * LANES] = pair_out[pr].astype(BF16)


def _sparse_attention(iq, iw, ik, aq, ak, av, batch, seq, tq):
    k_sel = min(TOPK_MAX, seq // 4)
    nq = seq // tq
    qrow = lambda w: pl.BlockSpec((tq, w), lambda b, j: (b * nq + j, 0))
    full = lambda w: pl.BlockSpec((seq, w), lambda b, j: (b, 0))
    return pl.pallas_call(
        functools.partial(_sparse_kernel, k_sel=k_sel, tq=tq),
        grid=(batch, nq),
        in_specs=[qrow(IDX_W), qrow(LANES), full(LANES), qrow(D_A), full(D_A), full(D_A)],
        out_specs=qrow(D_A),
        out_shape=jax.ShapeDtypeStruct((batch * seq, D_A), BF16),
        scratch_shapes=[pltpu.VMEM((nq, tq, tq), I32), pltpu.VMEM((nq, tq, tq), I32),
                        pltpu.VMEM((nq, tq, tq), F32)],
        compiler_params=_params("parallel", "arbitrary"),
        name="sparse_mixer",
    )(iq, iw, ik, aq, ak, av)


def _diff_kernel(dl_ref, sub_ref, bq_ref, bk_ref, bv_ref, o_ref, *, tq, lam_init):
    j = pl.program_id(1)
    lane = lax.broadcasted_iota(I32, (1, LANES), 1)
    low_half = lane < HEAD_DIM
    row_iota = lax.broadcasted_iota(I32, (tq, 1), 0)
    col_iota = lax.broadcasted_iota(I32, (1, tq), 1)
    diag_bias = jnp.where(col_iota <= row_iota, 0.0, MASKED)

    dl = dl_ref[...]
    lam = (jnp.exp(jnp.sum(dl[0:1] * dl[1:2], axis=1, keepdims=True))
           - jnp.exp(jnp.sum(dl[2:3] * dl[3:4], axis=1, keepdims=True)) + lam_init)

    for hh in range(B_HEADS):
        comps = []
        for cc in range(2):
            own = low_half if cc == 0 else jnp.logical_not(low_half)
            q = bq_ref[:, hh * LANES:(hh + 1) * LANES]
            q = jnp.where(own, q, jnp.zeros_like(q))

            def step(c, carry, bias, hh=hh, q=q):
                m, l, acc = carry
                start = pl.multiple_of(c * tq, tq)
                kc = bk_ref[pl.ds(start, tq), hh * LANES:(hh + 1) * LANES]
                vc = bv_ref[pl.ds(start, tq), hh * LANES:(hh + 1) * LANES]
                s = _dot_t(q, kc)
                if bias is not None:
                    s = s + bias
                m_new = jnp.maximum(m, jnp.max(s, axis=1, keepdims=True))
                alpha = jnp.exp(m - m_new)
                p = jnp.exp(s - m_new)
                l = alpha * l + jnp.sum(p, axis=1, keepdims=True)
                acc = alpha * acc + jnp.dot(p.astype(BF16), vc, preferred_element_type=F32)
                return m_new, l, acc

            init = (jnp.full((tq, 1), MASKED, F32), jnp.zeros((tq, 1), F32),
                    jnp.zeros((tq, LANES), F32))
            carry = lax.fori_loop(0, j, functools.partial(step, bias=None), init)
            _, l, acc = step(j, carry, diag_bias)
            comps.append(acc / l)
        y = comps[0] - lam * comps[1]
        ms = jnp.mean(y * y, axis=1, keepdims=True)
        y = y * lax.rsqrt(ms + EPS) * sub_ref[...] * (1.0 - lam_init)
        o_ref[:, hh * LANES:(hh + 1) * LANES] = y.astype(BF16)


def _diff_attention(diff_lambda, b_subln, bq, bk, bv, batch, seq, tq, layer_idx):
    nq = seq // tq
    lam_init = 0.8 - 0.6 * math.exp(-0.3 * layer_idx)
    qrow = lambda w: pl.BlockSpec((tq, w), lambda b, j: (b * nq + j, 0))
    full = lambda w: pl.BlockSpec((seq, w), lambda b, j: (b, 0))
    return pl.pallas_call(
        functools.partial(_diff_kernel, tq=tq, lam_init=lam_init),
        grid=(batch, nq),
        in_specs=[_const_spec((4, HEAD_DIM)), _const_spec((1, LANES)), qrow(D_B), full(D_B),
                  full(D_B)],
        out_specs=qrow(D_B),
        out_shape=jax.ShapeDtypeStruct((batch * seq, D_B), BF16),
        compiler_params=_params("parallel", "arbitrary"),
        name="diff_mixer",
    )(diff_lambda.astype(F32), b_subln.astype(F32)[None, :], bq, bk, bv)


def _sigmoid(z):
    return 1.0 / (1.0 + jnp.exp(-z))


def _out_ffn_kernel(x_ref, ya_ref, yb_ref, g_ref, gb_ref, wua_ref, wub_ref, wo_ref, fn_ref,
                    wg_ref, wu_ref, wd_ref, o_ref):
    d = x_ref.shape[1]
    ua = jnp.dot(ya_ref[...], wua_ref[...], preferred_element_type=F32)
    ub = jnp.dot(yb_ref[...], wub_ref[...], preferred_element_type=F32)
    g = _sigmoid(g_ref[...].astype(F32) + gb_ref[...])
    merged = g[:, :d] * ua + g[:, d:] * ub
    x1 = x_ref[...] + jnp.dot(merged.astype(BF16), wo_ref[...], preferred_element_type=F32)
    ms = jnp.mean(x1 * x1, axis=-1, keepdims=True)
    h = (x1 * lax.rsqrt(ms + EPS) * fn_ref[...]).astype(BF16)

    def chunk(c, acc):
        gate = jnp.dot(h, wg_ref[c], preferred_element_type=F32)
        up = jnp.dot(h, wu_ref[c], preferred_element_type=F32)
        act = (gate * _sigmoid(gate) * up).astype(BF16)
        return acc + jnp.dot(act, wd_ref[c], preferred_element_type=F32)

    o_ref[...] = lax.fori_loop(0, wg_ref.shape[0], chunk, x1)


def _out_ffn(x2, ya, yb, gates, gate_bias, w_up_a, w_up_b, w_out, ffn_norm, w_ffn_in, w_ffn_out,
             tm, fc):
    n, d = x2.shape
    hidden = w_ffn_out.shape[0]
    nc = hidden // fc
    wg = w_ffn_in[:, :hidden].reshape(d, nc, fc).transpose(1, 0, 2).astype(BF16)
    wu = w_ffn_in[:, hidden:].reshape(d, nc, fc).transpose(1, 0, 2).astype(BF16)
    wd = w_ffn_out.reshape(nc, fc, d).astype(BF16)
    row = lambda w: pl.BlockSpec((tm, w), lambda i: (i, 0))
    return pl.pallas_call(
        _out_ffn_kernel,
        grid=(n // tm,),
        in_specs=[row(d), row(D_A), row(D_B), row(2 * d), _const_spec((1, 2 * d)),
                  _const_spec((D_A, d)), _const_spec((D_B, d)), _const_spec((d, d)),
                  _const_spec((1, d)), _const_spec((nc, d, fc)), _const_spec((nc, d, fc)),
                  _const_spec((nc, fc, d))],
        out_specs=row(d),
        out_shape=jax.ShapeDtypeStruct((n, d), F32),
        compiler_params=_params("parallel"),
        name="out_ffn",
    )(x2, ya, yb, gates, gate_bias.astype(F32)[None, :], w_up_a.astype(BF16),
      w_up_b.astype(BF16), w_out.astype(BF16), ffn_norm.astype(F32)[None, :], wg, wu, wd)


def _tiles(batch, seq, hidden):
    n = batch * seq
    tm = 512 if n % 512 == 0 else seq
    tq = 256 if seq % 256 == 0 else seq
    fc = 256 if hidden % 256 == 0 else hidden
    return tm, tq, fc


def kernel(x, positions, attn_norm, w_in, gate_bias, a_q_norm, a_k_norm, idx_k_norm, b_q_norm, b_k_norm, diff_lambda, b_subln, w_up_a, w_up_b, w_out, ffn_norm, w_ffn_in, w_ffn_out):
    batch, seq, d = x.shape
    depth = w_in.shape[0]
    tm, tq, fc = _tiles(batch, seq, w_ffn_out.shape[1])
    cos, sin = _rope_tables(positions)
    x2 = x.reshape(batch * seq, d)
    for l in range(depth):
        aq, ak, av, iq, ik, iw, bq, bk, bv, gates = _in_projection(
            x2, cos, sin, attn_norm[l], w_in[l], a_q_norm[l], a_k_norm[l], idx_k_norm[l],
            b_q_norm[l], b_k_norm[l], tm)
        ya = _sparse_attention(iq, iw, ik, aq, ak, av, batch, seq, tq)
        yb = _diff_attention(diff_lambda[l], b_subln[l], bq, bk, bv, batch, seq, tq, l)
        x2 = _out_ffn(x2, ya, yb, gates, gate_bias[l], w_up_a[l], w_up_b[l], w_out[l],
                      ffn_norm[l], w_ffn_in[l], w_ffn_out[l], tm, fc)
    return x2.reshape(batch, seq, d)
```

```python
import functools
import math

import jax
import jax.numpy as jnp
from jax import lax
from jax.experimental import pallas as pl
from jax.experimental.pallas import tpu as pltpu

F32 = jnp.float32
BF16 = jnp.bfloat16
I32 = jnp.int32

HEAD_DIM = 64
HALF = HEAD_DIM // 2
A_HEADS = 8
IDX_HEADS = 4
B_HEADS = 4
TOPK_MAX = 256
ROPE_THETA = 10000.0
EPS = 1e-6
LANES = 128
MASKED = -1e30
INT_MIN = -(2 ** 31)
LOG2E = math.log2(math.e)
VMEM_LIMIT = 56 * 1024 * 1024

D_A = A_HEADS * HEAD_DIM
D_B = B_HEADS * 2 * HEAD_DIM
IDX_W = IDX_HEADS * HEAD_DIM

SEG_AQ = 0
SEG_AK = SEG_AQ + D_A
SEG_AV = SEG_AK + D_A
SEG_IQ = SEG_AV + D_A
SEG_IK = SEG_IQ + IDX_W
SEG_IW = SEG_IK + LANES
SEG_BQ = SEG_IW + LANES
SEG_BK = SEG_BQ + D_B
SEG_BV = SEG_BK + D_B
SEG_G = SEG_BV + D_B


def _const_spec(shape):
    nd = len(shape)
    return pl.BlockSpec(shape, lambda *_: (0,) * nd, pipeline_mode=pl.Buffered(1))


def _params(*sem):
    return pltpu.CompilerParams(dimension_semantics=sem, vmem_limit_bytes=VMEM_LIMIT)


def _trig_kernel(pos_ref, invf_ref, cos_ref, sin_ref):
    ang = pos_ref[...].astype(F32) * invf_ref[...]
    cos_ref[...] = jnp.cos(ang)
    sin_ref[...] = jnp.sin(ang)


def _rope_tables(positions):
    n = positions.size
    per_row = LANES // HALF
    rows = n // per_row
    inv_freq = 1.0 / (ROPE_THETA ** (jnp.arange(0, HEAD_DIM, 2, dtype=F32) / HEAD_DIM))
    invf = jnp.tile(inv_freq, per_row)[None, :]
    pos = jnp.repeat(positions.reshape(rows, per_row), HALF, axis=1)
    tr = min(rows, 2048)
    cos, sin = pl.pallas_call(
        _trig_kernel,
        grid=(rows // tr,),
        in_specs=[pl.BlockSpec((tr, LANES), lambda i: (i, 0)), _const_spec((1, LANES))],
        out_specs=[pl.BlockSpec((tr, LANES), lambda i: (i, 0))] * 2,
        out_shape=[jax.ShapeDtypeStruct((rows, LANES), F32)] * 2,
        compiler_params=_params("parallel"),
        name="rope_trig",
    )(pos, invf)
    cos = cos.reshape(n, HALF)
    sin = sin.reshape(n, HALF)
    cos = jnp.concatenate([cos, cos, cos, cos], axis=1)
    sin = jnp.concatenate([-sin, sin, -sin, sin], axis=1)
    return cos, sin


def _rope(y, cos, sin, first_half):
    cols = []
    for c in range(y.shape[1] // LANES):
        t = y[:, c * LANES:(c + 1) * LANES]
        partner = jnp.where(first_half, pltpu.roll(t, LANES - HALF, 1), pltpu.roll(t, HALF, 1))
        cols.append(t * cos + partner * sin)
    return cols[0] if len(cols) == 1 else jnp.concatenate(cols, axis=1)


def _head_sumsq(y, bd):
    sq = (y * y).astype(BF16)
    w = min(bd.shape[0], y.shape[1])
    bd = bd[:w, :w]
    cols = [jnp.dot(sq[:, c * w:(c + 1) * w], bd, preferred_element_type=F32)
            for c in range(y.shape[1] // w)]
    return cols[0] if len(cols) == 1 else jnp.concatenate(cols, axis=1)


def _inproj_kernel(x_ref, gn_ref, w_ref, bd_ref, gaq_ref, gak_ref, gik_ref, gbq_ref, gbk_ref,
                   cos_ref, sin_ref,
                   aq_ref, ak_ref, av_ref, iq_ref, ik_ref, iw_ref, bq_ref, bk_ref, bv_ref, g_ref):
    x = x_ref[...]
    ms = jnp.mean(x * x, axis=-1, keepdims=True)
    h = (x * lax.rsqrt(ms + EPS) * gn_ref[...]).astype(BF16)
    cos = cos_ref[...]
    sin = sin_ref[...]
    bd = bd_ref[...]
    lane = lax.broadcasted_iota(I32, (1, LANES), 1)
    first_half = (lane % HEAD_DIM) < HALF
    scale = HEAD_DIM ** -0.5

    def proj(start, width):
        return jnp.dot(h, w_ref[:, start:start + width], preferred_element_type=F32)

    def normed_rope(start, width, gain_ref, out_scale):
        y = proj(start, width)
        y = y * lax.rsqrt(_head_sumsq(y, bd) * (1.0 / HEAD_DIM) + EPS) * gain_ref[...]
        y = _rope(y, cos, sin, first_half)
        return y if out_scale is None else y * out_scale

    aq_ref[...] = normed_rope(SEG_AQ, D_A, gaq_ref, scale * LOG2E).astype(BF16)
    ak_ref[...] = normed_rope(SEG_AK, D_A, gak_ref, None).astype(BF16)
    av_ref[...] = proj(SEG_AV, D_A).astype(BF16)
    iq_ref[...] = _rope(proj(SEG_IQ, IDX_W), cos, sin, first_half) * scale
    ik_ref[...] = normed_rope(SEG_IK, LANES, gik_ref, None)
    iw_ref[...] = proj(SEG_IW, LANES) * (IDX_HEADS ** -0.5)
    bq_ref[...] = normed_rope(SEG_BQ, D_B, gbq_ref, scale * LOG2E).astype(BF16)
    bk_ref[...] = normed_rope(SEG_BK, D_B, gbk_ref, None).astype(BF16)
    bv_ref[...] = proj(SEG_BV, D_B).astype(BF16)
    g_ref[...] = proj(SEG_G, g_ref.shape[1]).astype(BF16)


def _pack_w_in(w_in, d_model):
    widths = (D_A, D_A, D_A, IDX_W, HEAD_DIM, IDX_HEADS, D_B, D_B, D_B, 2 * d_model)
    offs = [0]
    for wd in widths:
        offs.append(offs[-1] + wd)
    aq, ak, av, iq, ik, iw, bq, bk, bv, gates = [w_in[:, offs[i]:offs[i + 1]] for i in range(10)]
    iw = jnp.pad(iw, ((0, 0), (0, LANES - IDX_HEADS)))
    return jnp.concatenate([aq, ak, av, iq, ik, ik, iw, bq, bk, bv, gates], axis=1).astype(BF16)


def _in_projection(x2, cos, sin, attn_norm, w_in, a_q_norm, a_k_norm, idx_k_norm, b_q_norm,
                   b_k_norm, tm):
    n, d = x2.shape
    wp = _pack_w_in(w_in, d)
    dp = wp.shape[1]
    blk = 2 * LANES
    r = jnp.arange(blk)
    bd = (r[:, None] // HEAD_DIM == r[None, :] // HEAD_DIM).astype(BF16)

    def tiled(g, reps):
        return jnp.tile(g.astype(F32), reps)[None, :]

    row = lambda w: pl.BlockSpec((tm, w), lambda i: (i, 0))
    out_w = [(D_A, BF16), (D_A, BF16), (D_A, BF16), (IDX_W, F32), (LANES, F32), (LANES, F32),
             (D_B, BF16), (D_B, BF16), (D_B, BF16), (2 * d, BF16)]
    return pl.pallas_call(
        _inproj_kernel,
        grid=(n // tm,),
        in_specs=[row(d), _const_spec((1, d)), _const_spec((d, dp)), _const_spec((blk, blk)),
                  _const_spec((1, D_A)), _const_spec((1, D_A)), _const_spec((1, LANES)),
                  _const_spec((1, D_B)), _const_spec((1, D_B)), row(LANES), row(LANES)],
        out_specs=[row(w) for w, _ in out_w],
        out_shape=[jax.ShapeDtypeStruct((n, w), dt) for w, dt in out_w],
        compiler_params=_params("parallel"),
        name="in_projection",
    )(x2, attn_norm.astype(F32)[None, :], wp, bd,
      tiled(a_q_norm, A_HEADS), tiled(a_k_norm, A_HEADS), tiled(idx_k_norm, 2),
      tiled(b_q_norm, 2 * B_HEADS), tiled(b_k_norm, 2 * B_HEADS), cos, sin)


def _split_bf16(a):
    hi = a.astype(BF16)
    lo = (a - hi.astype(F32)).astype(BF16)
    return hi, lo


def _dot_t(a, b):
    return lax.dot_general(a, b, (((1,), (1,)), ((), ())), preferred_element_type=F32)


def _softmax_step(chain, q, k, v, bias, m_scr, l_scr, acc_scr):
    s = _dot_t(q, k)
    if bias is not None:
        s = s + bias
    m = m_scr[chain]
    m_new = jnp.maximum(m, jnp.max(s, axis=1, keepdims=True))
    alpha = jnp.exp2(m - m_new)
    p = jnp.exp2(s - m_new)
    l_scr[chain] = alpha * l_scr[chain] + jnp.sum(p, axis=1, keepdims=True)
    acc_scr[chain] = alpha * acc_scr[chain] + jnp.dot(p.astype(BF16), v,
                                                     preferred_element_type=F32)
    m_scr[chain] = m_new


def _init_softmax_state(m_scr, l_scr, acc_scr):
    m_scr[...] = jnp.full(m_scr.shape, MASKED, F32)
    l_scr[...] = jnp.zeros(l_scr.shape, F32)
    acc_scr[...] = jnp.zeros(acc_scr.shape, F32)


def _sparse_kernel(iq_ref, iw_ref, ik_ref, aq_ref, ak_ref, av_ref, o_ref,
                   key_scr, tie_scr, bias_scr, m_scr, l_scr, acc_scr, *, k_sel, tq):
    j = pl.program_id(1)
    n_chunks = j + 1
    nq = key_scr.shape[0]
    lane = lax.broadcasted_iota(I32, (1, LANES), 1)
    low_half = lane < HEAD_DIM
    row_pos = j * tq + lax.broadcasted_iota(I32, (tq, 1), 0)
    col_iota = lax.broadcasted_iota(I32, (1, tq), 1)

    iw = iw_ref[...]
    w_cols = [jnp.sum(jnp.where(lane == hh, iw, 0.0), axis=1, keepdims=True)
              for hh in range(IDX_HEADS)]
    iq_parts = []
    for hh in range(IDX_HEADS):
        pair = iq_ref[:, (hh // 2) * LANES:(hh // 2 + 1) * LANES]
        own = low_half if hh % 2 == 0 else jnp.logical_not(low_half)
        iq_parts.append(_split_bf16(jnp.where(own, pair, 0.0)))

    def score_chunk(c, carry):
        start = pl.multiple_of(c * tq, tq)
        k_hi, k_lo = _split_bf16(ik_ref[pl.ds(start, tq), :])
        score = jnp.zeros((tq, tq), F32)
        for hh in range(IDX_HEADS):
            q_hi, q_lo = iq_parts[hh]
            d = _dot_t(q_hi, k_hi) + _dot_t(q_hi, k_lo) + _dot_t(q_lo, k_hi)
            score = score + w_cols[hh] * jnp.maximum(d, 0.0)
        score = jnp.where(score == 0.0, 0.0, score)
        score = jnp.where(c * tq + col_iota <= row_pos, score, -jnp.inf)
        bits = lax.bitcast_convert_type(score, I32)
        key_scr[c] = bits ^ (lax.shift_right_arithmetic(bits, 31) & jnp.int32(0x7FFFFFFF))
        return carry

    lax.fori_loop(0, n_chunks, score_chunk, 0)

    def count(scr, pred):
        def body(c, acc):
            hit = jnp.where(pred(scr[c]), 1.0, 0.0)
            for t in range(tq // LANES):
                acc = acc + hit[:, t * LANES:(t + 1) * LANES]
            return acc
        acc = lax.fori_loop(0, n_chunks, body, jnp.zeros((tq, LANES), F32))
        return jnp.sum(acc, axis=1, keepdims=True)

    def key_bit(i, carry):
        t_u, c_ge = carry
        cand_u = t_u | lax.shift_left(jnp.int32(1), 31 - i)
        cand = cand_u ^ jnp.int32(INT_MIN)
        cnt = count(key_scr, lambda kk: kk >= cand)
        ok = cnt >= k_sel
        return jnp.where(ok, cand_u, t_u), jnp.where(ok, cnt, c_ge)

    all_keys = jnp.zeros((tq, 1), F32) + (n_chunks * tq).astype(F32)
    t_u, c_ge = lax.fori_loop(0, 32, key_bit, (jnp.zeros((tq, 1), I32), all_keys))
    tau = t_u ^ jnp.int32(INT_MIN)
    n_keys = nq * tq

    bias_scr[jnp.minimum(j + 1, nq - 1)] = jnp.full((tq, tq), MASKED, F32)
    tie_break = jnp.max(c_ge) > k_sel

    @pl.when(jnp.logical_not(tie_break))
    def _():
        def bias_chunk(c, carry):
            sel = (key_scr[c] >= tau) & (c * tq + col_iota <= row_pos)
            bias_scr[c] = jnp.where(sel, 0.0, MASKED)
            return carry
        lax.fori_loop(0, n_chunks, bias_chunk, 0)

    @pl.when(tie_break)
    def _():
        need = k_sel - count(key_scr, lambda kk: kk > tau)

        def tie_chunk(c, carry):
            tie_scr[c] = jnp.where(key_scr[c] == tau, c * tq + col_iota, n_keys)
            return carry
        lax.fori_loop(0, n_chunks, tie_chunk, 0)
        idx_bits = max(1, (n_keys - 1).bit_length())

        def idx_bit(i, p):
            cand = p | lax.shift_left(jnp.int32(1), idx_bits - 1 - i)
            cnt = count(tie_scr, lambda tt: tt < cand)
            return jnp.where(cnt < need, cand, p)
        p_tie = lax.fori_loop(0, idx_bits, idx_bit, jnp.zeros((tq, 1), I32))

        def bias_chunk(c, carry):
            sel = (key_scr[c] > tau) | (tie_scr[c] <= p_tie)
            sel = sel & (c * tq + col_iota <= row_pos)
            bias_scr[c] = jnp.where(sel, 0.0, MASKED)
            return carry
        lax.fori_loop(0, n_chunks, bias_chunk, 0)

    _init_softmax_state(m_scr, l_scr, acc_scr)

    def attend(c2, carry):
        start = pl.multiple_of(c2 * 2 * tq, 2 * tq)
        bias = jnp.concatenate([bias_scr[2 * c2], bias_scr[2 * c2 + 1]], axis=1)
        for pr in range(A_HEADS // 2):
            cols = slice(pr * LANES, (pr + 1) * LANES)
            q2 = aq_ref[:, cols]
            k2 = ak_ref[pl.ds(start, 2 * tq), cols]
            v2 = av_ref[pl.ds(start, 2 * tq), cols]
            for sub in range(2):
                own = low_half if sub == 0 else jnp.logical_not(low_half)
                q = jnp.where(own, q2, jnp.zeros_like(q2))
                v = jnp.where(own, v2, jnp.zeros_like(v2))
                _softmax_step(2 * pr + sub, q, k2, v, bias, m_scr, l_scr, acc_scr)
        return carry

    lax.fori_loop(0, (j + 2) // 2, attend, 0)
    for pr in range(A_HEADS // 2):
        out = acc_scr[2 * pr] / l_scr[2 * pr] + acc_scr[2 * pr + 1] / l_scr[2 * pr + 1]
        o_ref[:, pr * LANES:(pr + 1) * LANES] = out.astype(BF16)


def _sparse_attention(iq, iw, ik, aq, ak, av, batch, seq, tq):
    k_sel = min(TOPK_MAX, seq // 4)
    nq = seq // tq
    qrow = lambda w: pl.BlockSpec((tq, w), lambda b, j: (b * nq + j, 0))
    full = lambda w: pl.BlockSpec((seq, w), lambda b, j: (b, 0))
    return pl.pallas_call(
        functools.partial(_sparse_kernel, k_sel=k_sel, tq=tq),
        grid=(batch, nq),
        in_specs=[qrow(IDX_W), qrow(LANES), full(LANES), qrow(D_A), full(D_A), full(D_A)],
        out_specs=qrow(D_A),
        out_shape=jax.ShapeDtypeStruct((batch * seq, D_A), BF16),
        scratch_shapes=[pltpu.VMEM((nq, tq, tq), I32), pltpu.VMEM((nq, tq, tq), I32),
                        pltpu.VMEM((nq, tq, tq), F32),
                        pltpu.VMEM((A_HEADS, tq, 1), F32), pltpu.VMEM((A_HEADS, tq, 1), F32),
                        pltpu.VMEM((A_HEADS, tq, LANES), F32)],
        compiler_params=_params("parallel", "arbitrary"),
        name="sparse_mixer",
    )(iq, iw, ik, aq, ak, av)


def _diff_kernel(dl_ref, sub_ref, bq_ref, bk_ref, bv_ref, o_ref, m_scr, l_scr, acc_scr,
                 *, tq, lam_init):
    j = pl.program_id(1)
    lane = lax.broadcasted_iota(I32, (1, LANES), 1)
    low_half = lane < HEAD_DIM
    last = (j + 2) // 2 - 1

    _init_softmax_state(m_scr, l_scr, acc_scr)

    def all_chains(c2, bias):
        start = pl.multiple_of(c2 * 2 * tq, 2 * tq)
        for hh in range(B_HEADS):
            cols = slice(hh * LANES, (hh + 1) * LANES)
            q2 = bq_ref[:, cols]
            k2 = bk_ref[pl.ds(start, 2 * tq), cols]
            v2 = bv_ref[pl.ds(start, 2 * tq), cols]
            for cc in range(2):
                own = low_half if cc == 0 else jnp.logical_not(low_half)
                q = jnp.where(own, q2, jnp.zeros_like(q2))
                _softmax_step(2 * hh + cc, q, k2, v2, bias, m_scr, l_scr, acc_scr)

    def visible_block(c2, carry):
        all_chains(c2, None)
        return carry

    lax.fori_loop(0, last, visible_block, 0)
    key_pos = last * 2 * tq + lax.broadcasted_iota(I32, (1, 2 * tq), 1)
    row_pos = j * tq + lax.broadcasted_iota(I32, (tq, 1), 0)
    all_chains(last, jnp.where(key_pos <= row_pos, 0.0, MASKED))

    dl = dl_ref[...]
    lam = (jnp.exp(jnp.sum(dl[0:1] * dl[1:2], axis=1, keepdims=True))
           - jnp.exp(jnp.sum(dl[2:3] * dl[3:4], axis=1, keepdims=True)) + lam_init)
    for hh in range(B_HEADS):
        y = (acc_scr[2 * hh] / l_scr[2 * hh]
             - lam * (acc_scr[2 * hh + 1] / l_scr[2 * hh + 1]))
        ms = jnp.mean(y * y, axis=1, keepdims=True)
        y = y * lax.rsqrt(ms + EPS) * sub_ref[...] * (1.0 - lam_init)
        o_ref[:, hh * LANES:(hh + 1) * LANES] = y.astype(BF16)


def _diff_attention(diff_lambda, b_subln, bq, bk, bv, batch, seq, tq, layer_idx):
    nq = seq // tq
    lam_init = 0.8 - 0.6 * math.exp(-0.3 * layer_idx)
    chains = 2 * B_HEADS
    qrow = lambda w: pl.BlockSpec((tq, w), lambda b, j: (b * nq + j, 0))
    full = lambda w: pl.BlockSpec((seq, w), lambda b, j: (b, 0))
    return pl.pallas_call(
        functools.partial(_diff_kernel, tq=tq, lam_init=lam_init),
        grid=(batch, nq),
        in_specs=[_const_spec((4, HEAD_DIM)), _const_spec((1, LANES)), qrow(D_B), full(D_B),
                  full(D_B)],
        out_specs=qrow(D_B),
        out_shape=jax.ShapeDtypeStruct((batch * seq, D_B), BF16),
        scratch_shapes=[pltpu.VMEM((chains, tq, 1), F32), pltpu.VMEM((chains, tq, 1), F32),
                        pltpu.VMEM((chains, tq, LANES), F32)],
        compiler_params=_params("parallel", "arbitrary"),
        name="diff_mixer",
    )(diff_lambda.astype(F32), b_subln.astype(F32)[None, :], bq, bk, bv)


def _sigmoid(z):
    return 1.0 / (1.0 + jnp.exp(-z))


def _out_ffn_kernel(x_ref, ya_ref, yb_ref, g_ref, gb_ref, wua_ref, wub_ref, wo_ref, fn_ref,
                    wg_ref, wu_ref, wd_ref, o_ref, *, fc):
    d = x_ref.shape[1]
    ua = jnp.dot(ya_ref[...], wua_ref[...], preferred_element_type=F32)
    ub = jnp.dot(yb_ref[...], wub_ref[...], preferred_element_type=F32)
    g = _sigmoid(g_ref[...].astype(F32) + gb_ref[...])
    merged = g[:, :d] * ua + g[:, d:] * ub
    x1 = x_ref[...] + jnp.dot(merged.astype(BF16), wo_ref[...], preferred_element_type=F32)
    ms = jnp.mean(x1 * x1, axis=-1, keepdims=True)
    h = (x1 * lax.rsqrt(ms + EPS) * fn_ref[...]).astype(BF16)
    acc = x1
    for c in range(wd_ref.shape[0] // fc):
        cols = slice(c * fc, (c + 1) * fc)
        gate = jnp.dot(h, wg_ref[:, cols], preferred_element_type=F32)
        up = jnp.dot(h, wu_ref[:, cols], preferred_element_type=F32)
        act = (gate * _sigmoid(gate) * up).astype(BF16)
        acc = acc + jnp.dot(act, wd_ref[cols, :], preferred_element_type=F32)
    o_ref[...] = acc


def _out_ffn(x2, ya, yb, gates, gate_bias, w_up_a, w_up_b, w_out, ffn_norm, w_ffn_in, w_ffn_out,
             tm, fc):
    n, d = x2.shape
    hidden = w_ffn_out.shape[0]
    wg = w_ffn_in[:, :hidden].astype(BF16)
    wu = w_ffn_in[:, hidden:].astype(BF16)
    row = lambda w: pl.BlockSpec((tm, w), lambda i: (i, 0))
    return pl.pallas_call(
        functools.partial(_out_ffn_kernel, fc=fc),
        grid=(n // tm,),
        in_specs=[row(d), row(D_A), row(D_B), row(2 * d), _const_spec((1, 2 * d)),
                  _const_spec((D_A, d)), _const_spec((D_B, d)), _const_spec((d, d)),
                  _const_spec((1, d)), _const_spec((d, hidden)), _const_spec((d, hidden)),
                  _const_spec((hidden, d))],
        out_specs=row(d),
        out_shape=jax.ShapeDtypeStruct((n, d), F32),
        compiler_params=_params("parallel"),
        name="out_ffn",
    )(x2, ya, yb, gates, gate_bias.astype(F32)[None, :], w_up_a.astype(BF16),
      w_up_b.astype(BF16), w_out.astype(BF16), ffn_norm.astype(F32)[None, :], wg, wu,
      w_ffn_out.astype(BF16))


def _tiles(batch, seq, hidden):
    n = batch * seq
    tm = 512 if n % 512 == 0 else seq
    tq = 256
    assert seq % (2 * tq) == 0, "sequence length must be a multiple of 512"
    fc = 256 if hidden % 256 == 0 else hidden
    return tm, tq, fc


def kernel(x, positions, attn_norm, w_in, gate_bias, a_q_norm, a_k_norm, idx_k_norm, b_q_norm, b_k_norm, diff_lambda, b_subln, w_up_a, w_up_b, w_out, ffn_norm, w_ffn_in, w_ffn_out):
    batch, seq, d = x.shape
    depth = w_in.shape[0]
    tm, tq, fc = _tiles(batch, seq, w_ffn_out.shape[1])
    cos, sin = _rope_tables(positions)
    x2 = x.reshape(batch * seq, d)
    for l in range(depth):
        aq, ak, av, iq, ik, iw, bq, bk, bv, gates = _in_projection(
            x2, cos, sin, attn_norm[l], w_in[l], a_q_norm[l], a_k_norm[l], idx_k_norm[l],
            b_q_norm[l], b_k_norm[l], tm)
        ya = _sparse_attention(iq, iw, ik, aq, ak, av, batch, seq, tq)
        yb = _diff_attention(diff_lambda[l], b_subln[l], bq, bk, bv, batch, seq, tq, l)
        x2 = _out_ffn(x2, ya, yb, gates, gate_bias[l], w_up_a[l], w_up_b[l], w_out[l],
                      ffn_norm[l], w_ffn_in[l], w_ffn_out[l], tm, fc)
    return x2.reshape(batch, seq, d)
```

```python
import functools
import math

import jax
import jax.numpy as jnp
from jax import lax
from jax.experimental import pallas as pl
from jax.experimental.pallas import tpu as pltpu

F32 = jnp.float32
BF16 = jnp.bfloat16
I32 = jnp.int32

HEAD_DIM = 64
HALF = HEAD_DIM // 2
A_HEADS = 8
IDX_HEADS = 4
B_HEADS = 4
TOPK_MAX = 256
ROPE_THETA = 10000.0
EPS = 1e-6
LANES = 128
SUBLANES = 8
MASKED = -1e30
INT_MIN = -(2 ** 31)
LOG2E = math.log2(math.e)
VMEM_LIMIT = 56 * 1024 * 1024
AHEAD = 4

D_A = A_HEADS * HEAD_DIM
D_B = B_HEADS * 2 * HEAD_DIM
IDX_W = IDX_HEADS * HEAD_DIM

SEG_AQ = 0
SEG_AK = SEG_AQ + D_A
SEG_AV = SEG_AK + D_A
SEG_IQ = SEG_AV + D_A
SEG_IK = SEG_IQ + IDX_W
SEG_IW = SEG_IK + LANES
SEG_BQ = SEG_IW + LANES
SEG_BK = SEG_BQ + D_B
SEG_BV = SEG_BK + D_B
SEG_G = SEG_BV + D_B


def _const_spec(shape):
    nd = len(shape)
    return pl.BlockSpec(shape, lambda *_: (0,) * nd, pipeline_mode=pl.Buffered(1))


def _params(*sem):
    return pltpu.CompilerParams(dimension_semantics=sem, vmem_limit_bytes=VMEM_LIMIT)


def _trig_kernel(pos_ref, invf_ref, sign_ref, cos_ref, sin_ref):
    ang = pos_ref[...].astype(F32) * invf_ref[...]
    cos_ref[...] = jnp.cos(ang)
    sin_ref[...] = jnp.sin(ang) * sign_ref[...]


def _rope_tables(positions):
    n = positions.size
    inv_freq = 1.0 / (ROPE_THETA ** (jnp.arange(0, HEAD_DIM, 2, dtype=F32) / HEAD_DIM))
    invf = jnp.tile(inv_freq, LANES // HALF)[None, :]
    sign = jnp.tile(jnp.concatenate([-jnp.ones(HALF, F32), jnp.ones(HALF, F32)]),
                    LANES // HEAD_DIM)[None, :]
    tr = min(n, 2048)
    return pl.pallas_call(
        _trig_kernel,
        grid=(n // tr,),
        in_specs=[pl.BlockSpec((tr, 1), lambda i: (i, 0)), _const_spec((1, LANES)),
                  _const_spec((1, LANES))],
        out_specs=[pl.BlockSpec((tr, LANES), lambda i: (i, 0))] * 2,
        out_shape=[jax.ShapeDtypeStruct((n, LANES), F32)] * 2,
        compiler_params=_params("parallel"),
        name="rope_trig",
    )(positions.reshape(n, 1), invf, sign)


def _rope(y, cos, sin, first_half):
    cols = []
    for c in range(y.shape[1] // LANES):
        t = y[:, c * LANES:(c + 1) * LANES]
        partner = jnp.where(first_half, pltpu.roll(t, LANES - HALF, 1), pltpu.roll(t, HALF, 1))
        cols.append(t * cos + partner * sin)
    return cols[0] if len(cols) == 1 else jnp.concatenate(cols, axis=1)


def _head_sumsq(y, bd):
    sq = (y * y).astype(BF16)
    w = min(bd.shape[0], y.shape[1])
    bd = bd[:w, :w]
    cols = [jnp.dot(sq[:, c * w:(c + 1) * w], bd, preferred_element_type=F32)
            for c in range(y.shape[1] // w)]
    return cols[0] if len(cols) == 1 else jnp.concatenate(cols, axis=1)


def _inproj_kernel(x_ref, gn_ref, w_ref, bd_ref, gaq_ref, gak_ref, gik_ref, gbq_ref, gbk_ref,
                   cos_ref, sin_ref,
                   aq_ref, ak_ref, avt_ref, iq_ref, ik_ref, iw_ref, bq_ref, bk_ref, bvt_ref, g_ref):
    x = x_ref[...]
    ms = jnp.mean(x * x, axis=-1, keepdims=True)
    h = (x * lax.rsqrt(ms + EPS) * gn_ref[...]).astype(BF16)
    cos = cos_ref[...]
    sin = sin_ref[...]
    bd = bd_ref[...]
    lane = lax.broadcasted_iota(I32, (1, LANES), 1)
    first_half = (lane % HEAD_DIM) < HALF
    scale = HEAD_DIM ** -0.5

    def proj(start, width):
        return jnp.dot(h, w_ref[:, start:start + width], preferred_element_type=F32)

    def store_transposed(out_ref, y):
        tq = out_ref.shape[2]
        for s in range(out_ref.shape[0]):
            out_ref[s] = y[s * tq:(s + 1) * tq, :].T.astype(BF16)

    def normed_rope(start, width, gain_ref, out_scale):
        y = proj(start, width)
        y = y * lax.rsqrt(_head_sumsq(y, bd) * (1.0 / HEAD_DIM) + EPS) * gain_ref[...]
        y = _rope(y, cos, sin, first_half)
        return y if out_scale is None else y * out_scale

    aq_ref[...] = normed_rope(SEG_AQ, D_A, gaq_ref, scale * LOG2E).astype(BF16)
    ak_ref[...] = normed_rope(SEG_AK, D_A, gak_ref, None).astype(BF16)
    store_transposed(avt_ref, proj(SEG_AV, D_A))
    iq_ref[...] = _rope(proj(SEG_IQ, IDX_W), cos, sin, first_half) * scale
    ik_ref[...] = normed_rope(SEG_IK, LANES, gik_ref, None)
    iw_ref[...] = proj(SEG_IW, LANES) * (IDX_HEADS ** -0.5)
    bq_ref[...] = normed_rope(SEG_BQ, D_B, gbq_ref, scale * LOG2E).astype(BF16)
    bk_ref[...] = normed_rope(SEG_BK, D_B, gbk_ref, None).astype(BF16)
    store_transposed(bvt_ref, proj(SEG_BV, D_B))
    g_ref[...] = proj(SEG_G, g_ref.shape[1]).astype(BF16)


def _pack_w_in(w_in, d_model):
    widths = (D_A, D_A, D_A, IDX_W, HEAD_DIM, IDX_HEADS, D_B, D_B, D_B, 2 * d_model)
    offs = [0]
    for wd in widths:
        offs.append(offs[-1] + wd)
    aq, ak, av, iq, ik, iw, bq, bk, bv, gates = [w_in[:, offs[i]:offs[i + 1]] for i in range(10)]
    iw = jnp.pad(iw, ((0, 0), (0, LANES - IDX_HEADS)))
    return jnp.concatenate([aq, ak, av, iq, ik, ik, iw, bq, bk, bv, gates], axis=1).astype(BF16)


def _in_projection(x2, cos, sin, attn_norm, w_in, a_q_norm, a_k_norm, idx_k_norm, b_q_norm,
                   b_k_norm, tm, tq):
    n, d = x2.shape
    wp = _pack_w_in(w_in, d)
    dp = wp.shape[1]
    blk = 2 * LANES
    r = jnp.arange(blk)
    bd = (r[:, None] // HEAD_DIM == r[None, :] // HEAD_DIM).astype(BF16)

    def tiled(g, reps):
        return jnp.tile(g.astype(F32), reps)[None, :]

    row = lambda w: pl.BlockSpec((tm, w), lambda i: (i, 0))
    out_w = [(D_A, BF16), (D_A, BF16), None, (IDX_W, F32), (LANES, F32), (LANES, F32),
             (D_B, BF16), (D_B, BF16), None, (2 * d, BF16)]
    vt_spec = pl.BlockSpec((tm // tq, D_A, tq), lambda i: (i, 0, 0))
    vt_shape = jax.ShapeDtypeStruct((n // tq, D_A, tq), BF16)
    return pl.pallas_call(
        _inproj_kernel,
        grid=(n // tm,),
        in_specs=[row(d), _const_spec((1, d)), _const_spec((d, dp)), _const_spec((blk, blk)),
                  _const_spec((1, D_A)), _const_spec((1, D_A)), _const_spec((1, LANES)),
                  _const_spec((1, D_B)), _const_spec((1, D_B)), row(LANES), row(LANES)],
        out_specs=[vt_spec if o is None else row(o[0]) for o in out_w],
        out_shape=[vt_shape if o is None else jax.ShapeDtypeStruct((n, o[0]), o[1]) for o in out_w],
        compiler_params=_params("parallel"),
        name="in_projection",
    )(x2, attn_norm.astype(F32)[None, :], wp, bd,
      tiled(a_q_norm, A_HEADS), tiled(a_k_norm, A_HEADS), tiled(idx_k_norm, 2),
      tiled(b_q_norm, 2 * B_HEADS), tiled(b_k_norm, 2 * B_HEADS), cos, sin)


def _dot_t(a, b):
    return lax.dot_general(a, b, (((1,), (1,)), ((), ())), preferred_element_type=F32)


def _logits(q, k, bias):
    s = _dot_t(k, q)
    return s if bias is None else s + bias


def _softmax_update(state, s, vt):
    m, l, acc = state
    m_new = jnp.maximum(m, jnp.max(s, axis=0, keepdims=True))
    alpha = jnp.exp2(m - m_new)
    p = jnp.exp2(s - m_new)
    l = alpha * l + jnp.sum(p, axis=0, keepdims=True)
    acc = alpha * acc + jnp.dot(vt, p.astype(BF16), preferred_element_type=F32)
    return m_new, l, acc


def _advance_chains(states, steps):
    states = list(states)
    n = len(steps)
    logits = [None] * n
    for i in range(min(AHEAD, n)):
        logits[i] = steps[i][1]()
    for i in range(n):
        if i + AHEAD < n:
            logits[i + AHEAD] = steps[i + AHEAD][1]()
        chain, _, values_fn = steps[i]
        states[chain] = _softmax_update(states[chain], logits[i], values_fn())
        logits[i] = None
    return tuple(states)


def _stage_head_queries(q_ref, qm_scr, low_half):
    for hh in range(qm_scr.shape[0]):
        q2 = q_ref[:, (hh // 2) * LANES:(hh // 2 + 1) * LANES]
        own = low_half if hh % 2 == 0 else jnp.logical_not(low_half)
        qm_scr[hh] = jnp.where(own, q2, jnp.zeros_like(q2))


def _softmax_init(features, tq):
    return (jnp.full((1, tq), MASKED, F32), jnp.zeros((1, tq), F32),
            jnp.zeros((features, tq), F32))


def _sparse_kernel(iq_ref, iw_ref, ik_ref, aq_ref, ak_ref, avt_ref, o_ref,
                   key_scr, tie_scr, bias_scr, qm_scr, *, k_sel, tq):
    j = pl.program_id(1)
    n_chunks = j + 1
    nq = key_scr.shape[0]
    lane = lax.broadcasted_iota(I32, (1, LANES), 1)
    low_half = lane < HEAD_DIM
    q_pos = j * tq + lax.broadcasted_iota(I32, (1, tq), 1)
    key_iota = lax.broadcasted_iota(I32, (tq, 1), 0)

    w_t = iw_ref[...].T
    iq_heads = []
    for hh in range(IDX_HEADS):
        pair = iq_ref[:, (hh // 2) * LANES:(hh // 2 + 1) * LANES]
        own = low_half if hh % 2 == 0 else jnp.logical_not(low_half)
        iq_heads.append(jnp.where(own, pair, 0.0).astype(BF16))

    def score_chunk(c, carry):
        start = pl.multiple_of(c * tq, tq)
        ik = ik_ref[pl.ds(start, tq), :].astype(BF16)
        score = jnp.zeros((tq, tq), F32)
        for hh in range(IDX_HEADS):
            score = score + w_t[hh:hh + 1, :] * jnp.maximum(_dot_t(ik, iq_heads[hh]), 0.0)
        score = jnp.where(score == 0.0, 0.0, score)
        score = jnp.where(c * tq + key_iota <= q_pos, score, -jnp.inf)
        bits = lax.bitcast_convert_type(score, I32)
        key_scr[c] = bits ^ (lax.shift_right_arithmetic(bits, 31) & jnp.int32(0x7FFFFFFF))
        return carry

    lax.fori_loop(0, n_chunks, score_chunk, 0)

    def count(scr, pred):
        def body(c, acc):
            parts = [acc]
            for r in range(tq // SUBLANES):
                rows = scr[c, r * SUBLANES:(r + 1) * SUBLANES, :]
                parts.append(jnp.where(pred(rows), 1.0, 0.0))
            while len(parts) > 1:
                parts = [parts[i] + parts[i + 1] if i + 1 < len(parts) else parts[i]
                         for i in range(0, len(parts), 2)]
            return parts[0]
        acc = lax.fori_loop(0, n_chunks, body, jnp.zeros((SUBLANES, tq), F32))
        return jnp.sum(acc, axis=0, keepdims=True)

    def key_bit(i, carry):
        t_u, c_ge = carry
        cand_u = t_u | lax.shift_left(jnp.int32(1), 31 - i)
        cand = cand_u ^ jnp.int32(INT_MIN)
        cnt = count(key_scr, lambda kk: kk >= cand)
        ok = cnt >= k_sel
        return jnp.where(ok, cand_u, t_u), jnp.where(ok, cnt, c_ge)

    all_keys = jnp.zeros((1, tq), F32) + (n_chunks * tq).astype(F32)
    t_u, c_ge = lax.fori_loop(0, 32, key_bit, (jnp.zeros((1, tq), I32), all_keys))
    tau = t_u ^ jnp.int32(INT_MIN)
    n_keys = nq * tq
    tie_break = jnp.max(c_ge) > k_sel
    bias_scr[jnp.minimum(j + 1, nq - 1)] = jnp.full((tq, tq), MASKED, F32)

    @pl.when(jnp.logical_not(tie_break))
    def _():
        def bias_chunk(c, carry):
            sel = (key_scr[c] >= tau) & (c * tq + key_iota <= q_pos)
            bias_scr[c] = jnp.where(sel, 0.0, MASKED)
            return carry
        lax.fori_loop(0, n_chunks, bias_chunk, 0)

    @pl.when(tie_break)
    def _():
        need = k_sel - count(key_scr, lambda kk: kk > tau)

        def tie_chunk(c, carry):
            tie_scr[c] = jnp.where(key_scr[c] == tau, c * tq + key_iota, n_keys)
            return carry
        lax.fori_loop(0, n_chunks, tie_chunk, 0)
        idx_bits = max(1, (n_keys - 1).bit_length())

        def idx_bit(i, p):
            cand = p | lax.shift_left(jnp.int32(1), idx_bits - 1 - i)
            cnt = count(tie_scr, lambda tt: tt < cand)
            return jnp.where(cnt < need, cand, p)
        p_tie = lax.fori_loop(0, idx_bits, idx_bit, jnp.zeros((1, tq), I32))

        def bias_chunk(c, carry):
            sel = (key_scr[c] > tau) | (tie_scr[c] <= p_tie)
            sel = sel & (c * tq + key_iota <= q_pos)
            bias_scr[c] = jnp.where(sel, 0.0, MASKED)
            return carry
        lax.fori_loop(0, n_chunks, bias_chunk, 0)

    def chunk_steps(c):
        start = pl.multiple_of(c * tq, tq)
        steps = []
        for pr in range(A_HEADS // 2):
            cols = slice(pr * LANES, (pr + 1) * LANES)
            for half in range(2):
                hh = 2 * pr + half

                def chain_logits(cols=cols, hh=hh):
                    return _logits(qm_scr[hh], ak_ref[pl.ds(start, tq), cols], bias_scr[c])

                def chain_values(hh=hh):
                    return avt_ref[c, hh * HEAD_DIM:(hh + 1) * HEAD_DIM, :]
                steps.append((hh, chain_logits, chain_values))
        return steps

    def attend(c2, states):
        return _advance_chains(states, chunk_steps(2 * c2) + chunk_steps(2 * c2 + 1))

    _stage_head_queries(aq_ref, qm_scr, low_half)
    states = lax.fori_loop(0, (j + 2) // 2, attend,
                           tuple(_softmax_init(HEAD_DIM, tq) for _ in range(A_HEADS)))
    for pr in range(A_HEADS // 2):
        (_, l0, a0), (_, l1, a1) = states[2 * pr], states[2 * pr + 1]
        out_t = jnp.concatenate([a0 / l0, a1 / l1], axis=0)
        o_ref[:, pr * LANES:(pr + 1) * LANES] = out_t.T.astype(BF16)


def _sparse_attention(iq, iw, ik, aq, ak, avt, batch, seq, tq):
    k_sel = min(TOPK_MAX, seq // 4)
    nq = seq // tq
    qrow = lambda w: pl.BlockSpec((tq, w), lambda b, j: (b * nq + j, 0))
    full = lambda w: pl.BlockSpec((seq, w), lambda b, j: (b, 0))
    return pl.pallas_call(
        functools.partial(_sparse_kernel, k_sel=k_sel, tq=tq),
        grid=(batch, nq),
        in_specs=[qrow(IDX_W), qrow(LANES), full(LANES), qrow(D_A), full(D_A),
                  pl.BlockSpec((nq, D_A, tq), lambda b, j: (b, 0, 0))],
        out_specs=qrow(D_A),
        out_shape=jax.ShapeDtypeStruct((batch * seq, D_A), BF16),
        scratch_shapes=[pltpu.VMEM((nq, tq, tq), I32), pltpu.VMEM((nq, tq, tq), I32),
                        pltpu.VMEM((nq, tq, tq), F32), pltpu.VMEM((A_HEADS, tq, LANES), BF16)],
        compiler_params=_params("parallel", "arbitrary"),
        name="sparse_mixer",
    )(iq, iw, ik, aq, ak, avt)


def _diff_kernel(dl_ref, sub_ref, bq_ref, bk_ref, bvt_ref, o_ref, qm_scr, *, tq, lam_init):
    j = pl.program_id(1)
    lane = lax.broadcasted_iota(I32, (1, LANES), 1)
    low_half = lane < HEAD_DIM
    vdim = 2 * HEAD_DIM
    q_pos = j * tq + lax.broadcasted_iota(I32, (1, tq), 1)
    key_iota = lax.broadcasted_iota(I32, (tq, 1), 0)

    def chunk_steps(c, causal):
        start = pl.multiple_of(c * tq, tq)
        bias = jnp.where(c * tq + key_iota <= q_pos, 0.0, MASKED) if causal else None
        steps = []
        for hh in range(B_HEADS):
            cols = slice(hh * LANES, (hh + 1) * LANES)
            for cc in range(2):
                def chain_logits(cols=cols, ch=2 * hh + cc):
                    return _logits(qm_scr[ch], bk_ref[pl.ds(start, tq), cols], bias)

                def chain_values(hh=hh):
                    return bvt_ref[c, hh * vdim:(hh + 1) * vdim, :]
                steps.append((2 * hh + cc, chain_logits, chain_values))
        return steps

    def pair(c2, states, causal):
        return _advance_chains(states,
                               chunk_steps(2 * c2, causal) + chunk_steps(2 * c2 + 1, causal))

    last = (j + 2) // 2 - 1
    _stage_head_queries(bq_ref, qm_scr, low_half)
    init = tuple(_softmax_init(vdim, tq) for _ in range(2 * B_HEADS))
    states = lax.fori_loop(0, last, lambda c2, st: pair(c2, st, False), init)
    states = pair(last, states, True)

    dl = dl_ref[...]
    lam = (jnp.exp(jnp.sum(dl[0:1] * dl[1:2], axis=1, keepdims=True))
           - jnp.exp(jnp.sum(dl[2:3] * dl[3:4], axis=1, keepdims=True)) + lam_init)
    for hh in range(B_HEADS):
        (_, l0, a0), (_, l1, a1) = states[2 * hh], states[2 * hh + 1]
        y = a0 / l0 - lam * (a1 / l1)
        ms = jnp.mean(y * y, axis=0, keepdims=True)
        y = y * lax.rsqrt(ms + EPS) * sub_ref[...] * (1.0 - lam_init)
        o_ref[:, hh * LANES:(hh + 1) * LANES] = y.T.astype(BF16)


def _diff_attention(diff_lambda, b_subln, bq, bk, bvt, batch, seq, tq, layer_idx):
    nq = seq // tq
    lam_init = 0.8 - 0.6 * math.exp(-0.3 * layer_idx)
    subln = jnp.broadcast_to(b_subln.astype(F32)[:, None], (2 * HEAD_DIM, tq))
    qrow = lambda w: pl.BlockSpec((tq, w), lambda b, j: (b * nq + j, 0))
    full = lambda w: pl.BlockSpec((seq, w), lambda b, j: (b, 0))
    return pl.pallas_call(
        functools.partial(_diff_kernel, tq=tq, lam_init=lam_init),
        grid=(batch, nq),
        in_specs=[_const_spec((4, HEAD_DIM)), _const_spec((2 * HEAD_DIM, tq)), qrow(D_B),
                  full(D_B), pl.BlockSpec((nq, D_B, tq), lambda b, j: (b, 0, 0))],
        out_specs=qrow(D_B),
        out_shape=jax.ShapeDtypeStruct((batch * seq, D_B), BF16),
        scratch_shapes=[pltpu.VMEM((2 * B_HEADS, tq, LANES), BF16)],
        compiler_params=_params("parallel", "arbitrary"),
        name="diff_mixer",
    )(diff_lambda.astype(F32), subln, bq, bk, bvt)


def _sigmoid(z):
    return 1.0 / (1.0 + jnp.exp(-z))


def _out_ffn_kernel(x_ref, ya_ref, yb_ref, g_ref, gb_ref, wua_ref, wub_ref, wo_ref, fn_ref,
                    wg_ref, wu_ref, wd_ref, o_ref, *, fc):
    d = x_ref.shape[1]
    ua = jnp.dot(ya_ref[...], wua_ref[...], preferred_element_type=F32)
    ub = jnp.dot(yb_ref[...], wub_ref[...], preferred_element_type=F32)
    g = _sigmoid(g_ref[...].astype(F32) + gb_ref[...])
    merged = g[:, :d] * ua + g[:, d:] * ub
    x1 = x_ref[...] + jnp.dot(merged.astype(BF16), wo_ref[...], preferred_element_type=F32)
    ms = jnp.mean(x1 * x1, axis=-1, keepdims=True)
    h = (x1 * lax.rsqrt(ms + EPS) * fn_ref[...]).astype(BF16)
    acc = x1
    for c in range(wd_ref.shape[0] // fc):
        cols = slice(c * fc, (c + 1) * fc)
        gate = jnp.dot(h, wg_ref[:, cols], preferred_element_type=F32)
        up = jnp.dot(h, wu_ref[:, cols], preferred_element_type=F32)
        act = (gate * _sigmoid(gate) * up).astype(BF16)
        acc = acc + jnp.dot(act, wd_ref[cols, :], preferred_element_type=F32)
    o_ref[...] = acc


def _out_ffn(x2, ya, yb, gates, gate_bias, w_up_a, w_up_b, w_out, ffn_norm, w_ffn_in, w_ffn_out,
             tm, fc):
    n, d = x2.shape
    hidden = w_ffn_out.shape[0]
    wg = w_ffn_in[:, :hidden].astype(BF16)
    wu = w_ffn_in[:, hidden:].astype(BF16)
    row = lambda w: pl.BlockSpec((tm, w), lambda i: (i, 0))
    return pl.pallas_call(
        functools.partial(_out_ffn_kernel, fc=fc),
        grid=(n // tm,),
        in_specs=[row(d), row(D_A), row(D_B), row(2 * d), _const_spec((1, 2 * d)),
                  _const_spec((D_A, d)), _const_spec((D_B, d)), _const_spec((d, d)),
                  _const_spec((1, d)), _const_spec((d, hidden)), _const_spec((d, hidden)),
                  _const_spec((hidden, d))],
        out_specs=row(d),
        out_shape=jax.ShapeDtypeStruct((n, d), F32),
        compiler_params=_params("parallel"),
        name="out_ffn",
    )(x2, ya, yb, gates, gate_bias.astype(F32)[None, :], w_up_a.astype(BF16),
      w_up_b.astype(BF16), w_out.astype(BF16), ffn_norm.astype(F32)[None, :], wg, wu,
      w_ffn_out.astype(BF16))


def _tiles(batch, seq, hidden):
    n = batch * seq
    tm = 512 if n % 512 == 0 else seq
    tq = 256
    assert seq % (2 * tq) == 0 and n % tm == 0 and tm % tq == 0
    fc = 256 if hidden % 256 == 0 else hidden
    return tm, tq, fc


def kernel(x, positions, attn_norm, w_in, gate_bias, a_q_norm, a_k_norm, idx_k_norm, b_q_norm, b_k_norm, diff_lambda, b_subln, w_up_a, w_up_b, w_out, ffn_norm, w_ffn_in, w_ffn_out):
    batch, seq, d = x.shape
    depth = w_in.shape[0]
    tm, tq, fc = _tiles(batch, seq, w_ffn_out.shape[1])
    cos, sin = _rope_tables(positions)
    x2 = x.reshape(batch * seq, d)
    for l in range(depth):
        aq, ak, avt, iq, ik, iw, bq, bk, bvt, gates = _in_projection(
            x2, cos, sin, attn_norm[l], w_in[l], a_q_norm[l], a_k_norm[l], idx_k_norm[l],
            b_q_norm[l], b_k_norm[l], tm, tq)
        ya = _sparse_attention(iq, iw, ik, aq, ak, avt, batch, seq, tq)
        yb = _diff_attention(diff_lambda[l], b_subln[l], bq, bk, bvt, batch, seq, tq, l)
        x2 = _out_ffn(x2, ya, yb, gates, gate_bias[l], w_up_a[l], w_up_b[l], w_out[l],
                      ffn_norm[l], w_ffn_in[l], w_ffn_out[l], tm, fc)
    return x2.reshape(batch, seq, d)
```

```python
import functools
import math

import jax
import jax.numpy as jnp
from jax import lax
from jax.experimental import pallas as pl
from jax.experimental.pallas import tpu as pltpu

F32 = jnp.float32
BF16 = jnp.bfloat16
I32 = jnp.int32

HEAD_DIM = 64
HALF = HEAD_DIM // 2
A_HEADS = 8
IDX_HEADS = 4
B_HEADS = 4
TOPK_MAX = 256
ROPE_THETA = 10000.0
EPS = 1e-6
LANES = 128
SUBLANES = 8
MASKED = -1e30
INT_MIN = -(2 ** 31)
LOG2E = math.log2(math.e)
VMEM_LIMIT = 56 * 1024 * 1024
AHEAD = 6

D_A = A_HEADS * HEAD_DIM
D_B = B_HEADS * 2 * HEAD_DIM
IDX_W = IDX_HEADS * HEAD_DIM

SEG_AQ = 0
SEG_AK = SEG_AQ + D_A
SEG_AV = SEG_AK + D_A
SEG_IQ = SEG_AV + D_A
SEG_IK = SEG_IQ + IDX_W
SEG_IW = SEG_IK + LANES
SEG_BQ = SEG_IW + LANES
SEG_BK = SEG_BQ + D_B
SEG_BV = SEG_BK + D_B
SEG_G = SEG_BV + D_B


def _const_spec(shape):
    nd = len(shape)
    return pl.BlockSpec(shape, lambda *_: (0,) * nd, pipeline_mode=pl.Buffered(1))


def _params(*sem):
    return pltpu.CompilerParams(dimension_semantics=sem, vmem_limit_bytes=VMEM_LIMIT)


def _trig_kernel(pos_ref, invf_ref, sign_ref, cos_ref, sin_ref):
    ang = pos_ref[...].astype(F32) * invf_ref[...]
    cos_ref[...] = jnp.cos(ang)
    sin_ref[...] = jnp.sin(ang) * sign_ref[...]


def _rope_tables(positions):
    n = positions.size
    inv_freq = 1.0 / (ROPE_THETA ** (jnp.arange(0, HEAD_DIM, 2, dtype=F32) / HEAD_DIM))
    invf = jnp.tile(inv_freq, LANES // HALF)[None, :]
    sign = jnp.tile(jnp.concatenate([-jnp.ones(HALF, F32), jnp.ones(HALF, F32)]),
                    LANES // HEAD_DIM)[None, :]
    tr = min(n, 2048)
    return pl.pallas_call(
        _trig_kernel,
        grid=(n // tr,),
        in_specs=[pl.BlockSpec((tr, 1), lambda i: (i, 0)), _const_spec((1, LANES)),
                  _const_spec((1, LANES))],
        out_specs=[pl.BlockSpec((tr, LANES), lambda i: (i, 0))] * 2,
        out_shape=[jax.ShapeDtypeStruct((n, LANES), F32)] * 2,
        compiler_params=_params("parallel"),
        name="rope_trig",
    )(positions.reshape(n, 1), invf, sign)


def _rope(y, cos, sin, first_half):
    cols = []
    for c in range(y.shape[1] // LANES):
        t = y[:, c * LANES:(c + 1) * LANES]
        partner = jnp.where(first_half, pltpu.roll(t, LANES - HALF, 1), pltpu.roll(t, HALF, 1))
        cols.append(t * cos + partner * sin)
    return cols[0] if len(cols) == 1 else jnp.concatenate(cols, axis=1)


def _head_sumsq(y, bd):
    sq = (y * y).astype(BF16)
    w = min(bd.shape[0], y.shape[1])
    bd = bd[:w, :w]
    cols = [jnp.dot(sq[:, c * w:(c + 1) * w], bd, preferred_element_type=F32)
            for c in range(y.shape[1] // w)]
    return cols[0] if len(cols) == 1 else jnp.concatenate(cols, axis=1)


def _inproj_kernel(x_ref, gn_ref, w_ref, bd_ref, gaq_ref, gak_ref, gik_ref, gbq_ref, gbk_ref,
                   cos_ref, sin_ref,
                   aq_ref, ak_ref, avt_ref, iq_ref, ik_ref, iw_ref, bq_ref, bk_ref, bvt_ref, g_ref):
    x = x_ref[...]
    ms = jnp.mean(x * x, axis=-1, keepdims=True)
    h = (x * lax.rsqrt(ms + EPS) * gn_ref[...]).astype(BF16)
    cos = cos_ref[...]
    sin = sin_ref[...]
    bd = bd_ref[...]
    lane = lax.broadcasted_iota(I32, (1, LANES), 1)
    first_half = (lane % HEAD_DIM) < HALF
    scale = HEAD_DIM ** -0.5

    def proj(start, width):
        return jnp.dot(h, w_ref[:, start:start + width], preferred_element_type=F32)

    def store_transposed(out_ref, y):
        tq = out_ref.shape[2]
        for s in range(out_ref.shape[0]):
            out_ref[s] = y[s * tq:(s + 1) * tq, :].T.astype(BF16)

    def normed_rope(start, width, gain_ref, out_scale):
        y = proj(start, width)
        y = y * lax.rsqrt(_head_sumsq(y, bd) * (1.0 / HEAD_DIM) + EPS) * gain_ref[...]
        y = _rope(y, cos, sin, first_half)
        return y if out_scale is None else y * out_scale

    aq_ref[...] = normed_rope(SEG_AQ, D_A, gaq_ref, scale * LOG2E).astype(BF16)
    ak_ref[...] = normed_rope(SEG_AK, D_A, gak_ref, None).astype(BF16)
    store_transposed(avt_ref, proj(SEG_AV, D_A))
    iq_ref[...] = _rope(proj(SEG_IQ, IDX_W), cos, sin, first_half) * scale
    ik_ref[...] = normed_rope(SEG_IK, LANES, gik_ref, None)
    iw_ref[...] = proj(SEG_IW, LANES) * (IDX_HEADS ** -0.5)
    bq_ref[...] = normed_rope(SEG_BQ, D_B, gbq_ref, scale * LOG2E).astype(BF16)
    bk_ref[...] = normed_rope(SEG_BK, D_B, gbk_ref, None).astype(BF16)
    store_transposed(bvt_ref, proj(SEG_BV, D_B))
    g_ref[...] = proj(SEG_G, g_ref.shape[1]).astype(BF16)


def _pack_kernel(w_ref, o_ref, *, d_model):
    widths = (D_A, D_A, D_A, IDX_W, HEAD_DIM, IDX_HEADS, D_B, D_B, D_B, 2 * d_model)
    src = [0]
    for wd in widths:
        src.append(src[-1] + wd)
    dst = (SEG_AQ, SEG_AK, SEG_AV, SEG_IQ, None, None, SEG_BQ, SEG_BK, SEG_BV, SEG_G)
    for i, start in enumerate(dst):
        if start is not None:
            o_ref[0, :, start:start + widths[i]] = w_ref[0, :, src[i]:src[i + 1]].astype(BF16)
    ik = w_ref[0, :, src[4]:src[5]].astype(BF16)
    o_ref[0, :, SEG_IK:SEG_IK + HEAD_DIM] = ik
    o_ref[0, :, SEG_IK + HEAD_DIM:SEG_IK + LANES] = ik
    rows = o_ref.shape[1]
    o_ref[0, :, SEG_IW:SEG_IW + LANES] = jnp.zeros((rows, LANES), BF16)
    o_ref[0, :, SEG_IW:SEG_IW + IDX_HEADS] = w_ref[0, :, src[5]:src[6]].astype(BF16)


def _pack_w_in(w_in):
    depth, d, d_in = w_in.shape
    dp = SEG_G + 2 * d
    rows = 128 if d % 128 == 0 else d
    return pl.pallas_call(
        functools.partial(_pack_kernel, d_model=d),
        grid=(depth, d // rows),
        in_specs=[pl.BlockSpec((1, rows, d_in), lambda l, i: (l, i, 0))],
        out_specs=pl.BlockSpec((1, rows, dp), lambda l, i: (l, i, 0)),
        out_shape=jax.ShapeDtypeStruct((depth, d, dp), BF16),
        compiler_params=_params("parallel", "parallel"),
        name="pack_w_in",
    )(w_in)


def _in_projection(x2, cos, sin, attn_norm, wp, a_q_norm, a_k_norm, idx_k_norm, b_q_norm,
                   b_k_norm, tm, tq):
    n, d = x2.shape
    dp = wp.shape[1]
    blk = 2 * LANES
    r = jnp.arange(blk)
    bd = (r[:, None] // HEAD_DIM == r[None, :] // HEAD_DIM).astype(BF16)

    def tiled(g, reps):
        return jnp.tile(g.astype(F32), reps)[None, :]

    row = lambda w: pl.BlockSpec((tm, w), lambda i: (i, 0))
    out_w = [(D_A, BF16), (D_A, BF16), None, (IDX_W, F32), (LANES, F32), (LANES, F32),
             (D_B, BF16), (D_B, BF16), None, (2 * d, BF16)]
    vt_spec = pl.BlockSpec((tm // tq, D_A, tq), lambda i: (i, 0, 0))
    vt_shape = jax.ShapeDtypeStruct((n // tq, D_A, tq), BF16)
    return pl.pallas_call(
        _inproj_kernel,
        grid=(n // tm,),
        in_specs=[row(d), _const_spec((1, d)), _const_spec((d, dp)), _const_spec((blk, blk)),
                  _const_spec((1, D_A)), _const_spec((1, D_A)), _const_spec((1, LANES)),
                  _const_spec((1, D_B)), _const_spec((1, D_B)), row(LANES), row(LANES)],
        out_specs=[vt_spec if o is None else row(o[0]) for o in out_w],
        out_shape=[vt_shape if o is None else jax.ShapeDtypeStruct((n, o[0]), o[1]) for o in out_w],
        compiler_params=_params("parallel"),
        name="in_projection",
    )(x2, attn_norm.astype(F32)[None, :], wp, bd,
      tiled(a_q_norm, A_HEADS), tiled(a_k_norm, A_HEADS), tiled(idx_k_norm, 2),
      tiled(b_q_norm, 2 * B_HEADS), tiled(b_k_norm, 2 * B_HEADS), cos, sin)


def _dot_t(a, b):
    return lax.dot_general(a, b, (((1,), (1,)), ((), ())), preferred_element_type=F32)


def _logits(q, k, bias):
    s = _dot_t(k, q)
    return s if bias is None else s + bias


def _softmax_update(state, s, vt):
    m, l, acc = state
    m_new = jnp.maximum(m, jnp.max(s, axis=0, keepdims=True))
    alpha = jnp.exp2(m - m_new)
    p = jnp.exp2(s - m_new)
    l = alpha * l + jnp.sum(p, axis=0, keepdims=True)
    acc = alpha * acc + jnp.dot(vt, p.astype(BF16), preferred_element_type=F32)
    return m_new, l, acc


def _advance_chains(states, steps):
    states = list(states)
    n = len(steps)
    logits = [None] * n
    for i in range(min(AHEAD, n)):
        logits[i] = steps[i][1]()
    for i in range(n):
        if i + AHEAD < n:
            logits[i + AHEAD] = steps[i + AHEAD][1]()
        chain, _, values_fn = steps[i]
        states[chain] = _softmax_update(states[chain], logits[i], values_fn())
        logits[i] = None
    return tuple(states)


def _stage_head_queries(q_ref, qm_scr, low_half):
    for hh in range(qm_scr.shape[0]):
        q2 = q_ref[:, (hh // 2) * LANES:(hh // 2 + 1) * LANES]
        own = low_half if hh % 2 == 0 else jnp.logical_not(low_half)
        qm_scr[hh] = jnp.where(own, q2, jnp.zeros_like(q2))


def _softmax_init(features, tq):
    return (jnp.full((1, tq), MASKED, F32), jnp.zeros((1, tq), F32),
            jnp.zeros((features, tq), F32))


def _sparse_kernel(iq_ref, iw_ref, ik_ref, aq_ref, ak_ref, avt_ref, o_ref,
                   key_scr, tie_scr, bias_scr, qm_scr, *, k_sel, tq):
    j = pl.program_id(1)
    n_chunks = j + 1
    nq = key_scr.shape[0]
    lane = lax.broadcasted_iota(I32, (1, LANES), 1)
    low_half = lane < HEAD_DIM
    q_pos = j * tq + lax.broadcasted_iota(I32, (1, tq), 1)
    key_iota = lax.broadcasted_iota(I32, (tq, 1), 0)

    w_t = iw_ref[...].T
    iq_heads = []
    for hh in range(IDX_HEADS):
        pair = iq_ref[:, (hh // 2) * LANES:(hh // 2 + 1) * LANES]
        own = low_half if hh % 2 == 0 else jnp.logical_not(low_half)
        iq_heads.append(jnp.where(own, pair, 0.0).astype(BF16))

    def score_chunk(c, carry):
        start = pl.multiple_of(c * tq, tq)
        ik = ik_ref[pl.ds(start, tq), :].astype(BF16)
        score = jnp.zeros((tq, tq), F32)
        for hh in range(IDX_HEADS):
            score = score + w_t[hh:hh + 1, :] * jnp.maximum(_dot_t(ik, iq_heads[hh]), 0.0)
        score = jnp.where(score == 0.0, 0.0, score)
        score = jnp.where(c * tq + key_iota <= q_pos, score, -jnp.inf)
        bits = lax.bitcast_convert_type(score, I32)
        key_scr[c] = bits ^ (lax.shift_right_arithmetic(bits, 31) & jnp.int32(0x7FFFFFFF))
        return carry

    lax.fori_loop(0, n_chunks, score_chunk, 0)

    def count(scr, pred):
        def body(c, acc):
            parts = [acc]
            for r in range(tq // SUBLANES):
                rows = scr[c, r * SUBLANES:(r + 1) * SUBLANES, :]
                parts.append(jnp.where(pred(rows), 1.0, 0.0))
            while len(parts) > 1:
                parts = [parts[i] + parts[i + 1] if i + 1 < len(parts) else parts[i]
                         for i in range(0, len(parts), 2)]
            return parts[0]
        acc = lax.fori_loop(0, n_chunks, body, jnp.zeros((SUBLANES, tq), F32))
        return jnp.sum(acc, axis=0, keepdims=True)

    def key_bit(i, carry):
        t_u, c_ge = carry
        cand_u = t_u | lax.shift_left(jnp.int32(1), 31 - i)
        cand = cand_u ^ jnp.int32(INT_MIN)
        cnt = count(key_scr, lambda kk: kk >= cand)
        ok = cnt >= k_sel
        return jnp.where(ok, cand_u, t_u), jnp.where(ok, cnt, c_ge)

    all_keys = jnp.zeros((1, tq), F32) + (n_chunks * tq).astype(F32)
    t_u, c_ge = lax.fori_loop(0, 32, key_bit, (jnp.zeros((1, tq), I32), all_keys))
    tau = t_u ^ jnp.int32(INT_MIN)
    n_keys = nq * tq
    tie_break = jnp.max(c_ge) > k_sel
    bias_scr[jnp.minimum(j + 1, nq - 1)] = jnp.full((tq, tq), MASKED, F32)

    @pl.when(jnp.logical_not(tie_break))
    def _():
        def bias_chunk(c, carry):
            sel = (key_scr[c] >= tau) & (c * tq + key_iota <= q_pos)
            bias_scr[c] = jnp.where(sel, 0.0, MASKED)
            return carry
        lax.fori_loop(0, n_chunks, bias_chunk, 0)

    @pl.when(tie_break)
    def _():
        need = k_sel - count(key_scr, lambda kk: kk > tau)

        def tie_chunk(c, carry):
            tie_scr[c] = jnp.where(key_scr[c] == tau, c * tq + key_iota, n_keys)
            return carry
        lax.fori_loop(0, n_chunks, tie_chunk, 0)
        idx_bits = max(1, (n_keys - 1).bit_length())

        def idx_bit(i, p):
            cand = p | lax.shift_left(jnp.int32(1), idx_bits - 1 - i)
            cnt = count(tie_scr, lambda tt: tt < cand)
            return jnp.where(cnt < need, cand, p)
        p_tie = lax.fori_loop(0, idx_bits, idx_bit, jnp.zeros((1, tq), I32))

        def bias_chunk(c, carry):
            sel = (key_scr[c] > tau) | (tie_scr[c] <= p_tie)
            sel = sel & (c * tq + key_iota <= q_pos)
            bias_scr[c] = jnp.where(sel, 0.0, MASKED)
            return carry
        lax.fori_loop(0, n_chunks, bias_chunk, 0)

    def chunk_steps(c):
        start = pl.multiple_of(c * tq, tq)
        steps = []
        for pr in range(A_HEADS // 2):
            cols = slice(pr * LANES, (pr + 1) * LANES)
            for half in range(2):
                hh = 2 * pr + half

                def chain_logits(cols=cols, hh=hh):
                    return _logits(qm_scr[hh], ak_ref[pl.ds(start, tq), cols], bias_scr[c])

                def chain_values(hh=hh):
                    return avt_ref[c, hh * HEAD_DIM:(hh + 1) * HEAD_DIM, :]
                steps.append((hh, chain_logits, chain_values))
        return steps

    def attend(c2, states):
        return _advance_chains(states, chunk_steps(2 * c2) + chunk_steps(2 * c2 + 1))

    _stage_head_queries(aq_ref, qm_scr, low_half)
    states = lax.fori_loop(0, (j + 2) // 2, attend,
                           tuple(_softmax_init(HEAD_DIM, tq) for _ in range(A_HEADS)))
    for pr in range(A_HEADS // 2):
        (_, l0, a0), (_, l1, a1) = states[2 * pr], states[2 * pr + 1]
        out_t = jnp.concatenate([a0 / l0, a1 / l1], axis=0)
        o_ref[:, pr * LANES:(pr + 1) * LANES] = out_t.T.astype(BF16)


def _sparse_attention(iq, iw, ik, aq, ak, avt, batch, seq, tq):
    k_sel = min(TOPK_MAX, seq // 4)
    nq = seq // tq
    qrow = lambda w: pl.BlockSpec((tq, w), lambda b, j: (b * nq + j, 0))
    full = lambda w: pl.BlockSpec((seq, w), lambda b, j: (b, 0))
    return pl.pallas_call(
        functools.partial(_sparse_kernel, k_sel=k_sel, tq=tq),
        grid=(batch, nq),
        in_specs=[qrow(IDX_W), qrow(LANES), full(LANES), qrow(D_A), full(D_A),
                  pl.BlockSpec((nq, D_A, tq), lambda b, j: (b, 0, 0))],
        out_specs=qrow(D_A),
        out_shape=jax.ShapeDtypeStruct((batch * seq, D_A), BF16),
        scratch_shapes=[pltpu.VMEM((nq, tq, tq), I32), pltpu.VMEM((nq, tq, tq), I32),
                        pltpu.VMEM((nq, tq, tq), F32), pltpu.VMEM((A_HEADS, tq, LANES), BF16)],
        compiler_params=_params("parallel", "arbitrary"),
        name="sparse_mixer",
    )(iq, iw, ik, aq, ak, avt)


def _diff_kernel(dl_ref, sub_ref, bq_ref, bk_ref, bvt_ref, o_ref, qm_scr, *, tq, lam_init):
    j = pl.program_id(1)
    lane = lax.broadcasted_iota(I32, (1, LANES), 1)
    low_half = lane < HEAD_DIM
    vdim = 2 * HEAD_DIM
    q_pos = j * tq + lax.broadcasted_iota(I32, (1, tq), 1)
    key_iota = lax.broadcasted_iota(I32, (tq, 1), 0)

    def chunk_steps(c, causal):
        start = pl.multiple_of(c * tq, tq)
        bias = jnp.where(c * tq + key_iota <= q_pos, 0.0, MASKED) if causal else None
        steps = []
        for hh in range(B_HEADS):
            cols = slice(hh * LANES, (hh + 1) * LANES)
            for cc in range(2):
                def chain_logits(cols=cols, ch=2 * hh + cc):
                    return _logits(qm_scr[ch], bk_ref[pl.ds(start, tq), cols], bias)

                def chain_values(hh=hh):
                    return bvt_ref[c, hh * vdim:(hh + 1) * vdim, :]
                steps.append((2 * hh + cc, chain_logits, chain_values))
        return steps

    def pair(c2, states, causal):
        return _advance_chains(states,
                               chunk_steps(2 * c2, causal) + chunk_steps(2 * c2 + 1, causal))

    last = (j + 2) // 2 - 1
    _stage_head_queries(bq_ref, qm_scr, low_half)
    init = tuple(_softmax_init(vdim, tq) for _ in range(2 * B_HEADS))
    states = lax.fori_loop(0, last, lambda c2, st: pair(c2, st, False), init)
    states = pair(last, states, True)

    dl = dl_ref[...]
    lam = (jnp.exp(jnp.sum(dl[0:1] * dl[1:2], axis=1, keepdims=True))
           - jnp.exp(jnp.sum(dl[2:3] * dl[3:4], axis=1, keepdims=True)) + lam_init)
    for hh in range(B_HEADS):
        (_, l0, a0), (_, l1, a1) = states[2 * hh], states[2 * hh + 1]
        y = a0 / l0 - lam * (a1 / l1)
        ms = jnp.mean(y * y, axis=0, keepdims=True)
        y = y * lax.rsqrt(ms + EPS) * sub_ref[...] * (1.0 - lam_init)
        o_ref[:, hh * LANES:(hh + 1) * LANES] = y.T.astype(BF16)


def _diff_attention(diff_lambda, b_subln, bq, bk, bvt, batch, seq, tq, layer_idx):
    nq = seq // tq
    lam_init = 0.8 - 0.6 * math.exp(-0.3 * layer_idx)
    subln = jnp.broadcast_to(b_subln.astype(F32)[:, None], (2 * HEAD_DIM, tq))
    qrow = lambda w: pl.BlockSpec((tq, w), lambda b, j: (b * nq + j, 0))
    full = lambda w: pl.BlockSpec((seq, w), lambda b, j: (b, 0))
    return pl.pallas_call(
        functools.partial(_diff_kernel, tq=tq, lam_init=lam_init),
        grid=(batch, nq),
        in_specs=[_const_spec((4, HEAD_DIM)), _const_spec((2 * HEAD_DIM, tq)), qrow(D_B),
                  full(D_B), pl.BlockSpec((nq, D_B, tq), lambda b, j: (b, 0, 0))],
        out_specs=qrow(D_B),
        out_shape=jax.ShapeDtypeStruct((batch * seq, D_B), BF16),
        scratch_shapes=[pltpu.VMEM((2 * B_HEADS, tq, LANES), BF16)],
        compiler_params=_params("parallel", "arbitrary"),
        name="diff_mixer",
    )(diff_lambda.astype(F32), subln, bq, bk, bvt)


def _sigmoid(z):
    return 1.0 / (1.0 + jnp.exp(-z))


def _out_ffn_kernel(x_ref, ya_ref, yb_ref, g_ref, gb_ref, wua_ref, wub_ref, wo_ref, fn_ref,
                    wg_ref, wu_ref, wd_ref, o_ref, *, fc):
    d = x_ref.shape[1]
    ua = jnp.dot(ya_ref[...], wua_ref[...], preferred_element_type=F32)
    ub = jnp.dot(yb_ref[...], wub_ref[...], preferred_element_type=F32)
    g = _sigmoid(g_ref[...].astype(F32) + gb_ref[...])
    merged = g[:, :d] * ua + g[:, d:] * ub
    x1 = x_ref[...] + jnp.dot(merged.astype(BF16), wo_ref[...], preferred_element_type=F32)
    ms = jnp.mean(x1 * x1, axis=-1, keepdims=True)
    h = (x1 * lax.rsqrt(ms + EPS) * fn_ref[...]).astype(BF16)
    acc = x1
    for c in range(wd_ref.shape[0] // fc):
        cols = slice(c * fc, (c + 1) * fc)
        gate = jnp.dot(h, wg_ref[:, cols], preferred_element_type=F32)
        up = jnp.dot(h, wu_ref[:, cols], preferred_element_type=F32)
        act = (gate * _sigmoid(gate) * up).astype(BF16)
        acc = acc + jnp.dot(act, wd_ref[cols, :], preferred_element_type=F32)
    o_ref[...] = acc


def _out_ffn(x2, ya, yb, gates, gate_bias, w_up_a, w_up_b, w_out, ffn_norm, w_ffn_in, w_ffn_out,
             tm, fc):
    n, d = x2.shape
    hidden = w_ffn_out.shape[0]
    wg = w_ffn_in[:, :hidden].astype(BF16)
    wu = w_ffn_in[:, hidden:].astype(BF16)
    row = lambda w: pl.BlockSpec((tm, w), lambda i: (i, 0))
    return pl.pallas_call(
        functools.partial(_out_ffn_kernel, fc=fc),
        grid=(n // tm,),
        in_specs=[row(d), row(D_A), row(D_B), row(2 * d), _const_spec((1, 2 * d)),
                  _const_spec((D_A, d)), _const_spec((D_B, d)), _const_spec((d, d)),
                  _const_spec((1, d)), _const_spec((d, hidden)), _const_spec((d, hidden)),
                  _const_spec((hidden, d))],
        out_specs=row(d),
        out_shape=jax.ShapeDtypeStruct((n, d), F32),
        compiler_params=_params("parallel"),
        name="out_ffn",
    )(x2, ya, yb, gates, gate_bias.astype(F32)[None, :], w_up_a.astype(BF16),
      w_up_b.astype(BF16), w_out.astype(BF16), ffn_norm.astype(F32)[None, :], wg, wu,
      w_ffn_out.astype(BF16))


def _tiles(batch, seq, hidden):
    n = batch * seq
    tm = 512 if n % 512 == 0 else seq
    tq = 256
    assert seq % (2 * tq) == 0 and n % tm == 0 and tm % tq == 0
    fc = 256 if hidden % 256 == 0 else hidden
    return tm, tq, fc


def kernel(x, positions, attn_norm, w_in, gate_bias, a_q_norm, a_k_norm, idx_k_norm, b_q_norm, b_k_norm, diff_lambda, b_subln, w_up_a, w_up_b, w_out, ffn_norm, w_ffn_in, w_ffn_out):
    batch, seq, d = x.shape
    depth = w_in.shape[0]
    tm, tq, fc = _tiles(batch, seq, w_ffn_out.shape[1])
    cos, sin = _rope_tables(positions)
    w_in_packed = _pack_w_in(w_in)
    x2 = x.reshape(batch * seq, d)
    for l in range(depth):
        aq, ak, avt, iq, ik, iw, bq, bk, bvt, gates = _in_projection(
            x2, cos, sin, attn_norm[l], w_in_packed[l], a_q_norm[l], a_k_norm[l], idx_k_norm[l],
            b_q_norm[l], b_k_norm[l], tm, tq)
        ya = _sparse_attention(iq, iw, ik, aq, ak, avt, batch, seq, tq)
        yb = _diff_attention(diff_lambda[l], b_subln[l], bq, bk, bvt, batch, seq, tq, l)
        x2 = _out_ffn(x2, ya, yb, gates, gate_bias[l], w_up_a[l], w_up_b[l], w_out[l],
                      ffn_norm[l], w_ffn_in[l], w_ffn_out[l], tm, fc)
    return x2.reshape(batch, seq, d)
```

```python
import functools
import math

import jax
import jax.numpy as jnp
from jax import lax
from jax.experimental import pallas as pl
from jax.experimental.pallas import tpu as pltpu

F32 = jnp.float32
BF16 = jnp.bfloat16
I32 = jnp.int32

HEAD_DIM = 64
HALF = HEAD_DIM // 2
A_HEADS = 8
IDX_HEADS = 4
B_HEADS = 4
TOPK_MAX = 256
ROPE_THETA = 10000.0
EPS = 1e-6
LANES = 128
SUBLANES = 8
MASKED = -1e30
INT_MIN = -(2 ** 31)
LOG2E = math.log2(math.e)
VMEM_LIMIT = 56 * 1024 * 1024
ONES_ROWS = 16
AHEAD = 6

D_A = A_HEADS * HEAD_DIM
D_B = B_HEADS * 2 * HEAD_DIM
IDX_W = IDX_HEADS * HEAD_DIM

SEG_AQ = 0
SEG_AK = SEG_AQ + D_A
SEG_AV = SEG_AK + D_A
SEG_IQ = SEG_AV + D_A
SEG_IK = SEG_IQ + IDX_W
SEG_IW = SEG_IK + LANES
SEG_BQ = SEG_IW + LANES
SEG_BK = SEG_BQ + D_B
SEG_BV = SEG_BK + D_B
SEG_G = SEG_BV + D_B


def _const_spec(shape):
    nd = len(shape)
    return pl.BlockSpec(shape, lambda *_: (0,) * nd, pipeline_mode=pl.Buffered(1))


def _params(*sem):
    return pltpu.CompilerParams(dimension_semantics=sem, vmem_limit_bytes=VMEM_LIMIT)


def _trig_kernel(pos_ref, invf_ref, sign_ref, cos_ref, sin_ref):
    ang = pos_ref[...].astype(F32) * invf_ref[...]
    cos_ref[...] = jnp.cos(ang)
    sin_ref[...] = jnp.sin(ang) * sign_ref[...]


def _rope_tables(positions):
    n = positions.size
    inv_freq = 1.0 / (ROPE_THETA ** (jnp.arange(0, HEAD_DIM, 2, dtype=F32) / HEAD_DIM))
    invf = jnp.tile(inv_freq, LANES // HALF)[None, :]
    sign = jnp.tile(jnp.concatenate([-jnp.ones(HALF, F32), jnp.ones(HALF, F32)]),
                    LANES // HEAD_DIM)[None, :]
    tr = min(n, 2048)
    return pl.pallas_call(
        _trig_kernel,
        grid=(n // tr,),
        in_specs=[pl.BlockSpec((tr, 1), lambda i: (i, 0)), _const_spec((1, LANES)),
                  _const_spec((1, LANES))],
        out_specs=[pl.BlockSpec((tr, LANES), lambda i: (i, 0))] * 2,
        out_shape=[jax.ShapeDtypeStruct((n, LANES), F32)] * 2,
        compiler_params=_params("parallel"),
        name="rope_trig",
    )(positions.reshape(n, 1), invf, sign)


def _rope(y, cos, sin, first_half):
    cols = []
    for c in range(y.shape[1] // LANES):
        t = y[:, c * LANES:(c + 1) * LANES]
        partner = jnp.where(first_half, pltpu.roll(t, LANES - HALF, 1), pltpu.roll(t, HALF, 1))
        cols.append(t * cos + partner * sin)
    return cols[0] if len(cols) == 1 else jnp.concatenate(cols, axis=1)


def _head_sumsq(y, bd):
    sq = (y * y).astype(BF16)
    w = min(bd.shape[0], y.shape[1])
    bd = bd[:w, :w]
    cols = [jnp.dot(sq[:, c * w:(c + 1) * w], bd, preferred_element_type=F32)
            for c in range(y.shape[1] // w)]
    return cols[0] if len(cols) == 1 else jnp.concatenate(cols, axis=1)


def _inproj_kernel(x_ref, gn_ref, w_ref, bd_ref, gaq_ref, gak_ref, gik_ref, gbq_ref, gbk_ref,
                   cos_ref, sin_ref,
                   aq_ref, ak_ref, avt_ref, iq_ref, ik_ref, iw_ref, bq_ref, bk_ref, bvt_ref, g_ref):
    x = x_ref[...]
    ms = jnp.mean(x * x, axis=-1, keepdims=True)
    h = (x * lax.rsqrt(ms + EPS) * gn_ref[...]).astype(BF16)
    cos = cos_ref[...]
    sin = sin_ref[...]
    bd = bd_ref[...]
    lane = lax.broadcasted_iota(I32, (1, LANES), 1)
    first_half = (lane % HEAD_DIM) < HALF
    scale = HEAD_DIM ** -0.5

    def proj(start, width):
        return jnp.dot(h, w_ref[:, start:start + width], preferred_element_type=F32)

    def store_transposed(out_ref, y):
        tq = out_ref.shape[2]
        for s in range(out_ref.shape[0]):
            out_ref[s] = y[s * tq:(s + 1) * tq, :].T.astype(BF16)

    def normed_rope(start, width, gain_ref, out_scale):
        y = proj(start, width)
        y = y * lax.rsqrt(_head_sumsq(y, bd) * (1.0 / HEAD_DIM) + EPS) * gain_ref[...]
        y = _rope(y, cos, sin, first_half)
        return y if out_scale is None else y * out_scale

    aq_ref[...] = normed_rope(SEG_AQ, D_A, gaq_ref, scale * LOG2E).astype(BF16)
    ak_ref[...] = normed_rope(SEG_AK, D_A, gak_ref, None).astype(BF16)
    store_transposed(avt_ref, proj(SEG_AV, D_A))
    iq_ref[...] = _rope(proj(SEG_IQ, IDX_W), cos, sin, first_half) * scale
    ik_ref[...] = normed_rope(SEG_IK, LANES, gik_ref, None)
    iw_ref[...] = proj(SEG_IW, LANES) * (IDX_HEADS ** -0.5)
    bq_ref[...] = normed_rope(SEG_BQ, D_B, gbq_ref, scale * LOG2E).astype(BF16)
    bk_ref[...] = normed_rope(SEG_BK, D_B, gbk_ref, None).astype(BF16)
    store_transposed(bvt_ref, proj(SEG_BV, D_B))
    g_ref[...] = proj(SEG_G, g_ref.shape[1]).astype(BF16)


def _pack_kernel(w_ref, o_ref, *, d_model):
    widths = (D_A, D_A, D_A, IDX_W, HEAD_DIM, IDX_HEADS, D_B, D_B, D_B, 2 * d_model)
    src = [0]
    for wd in widths:
        src.append(src[-1] + wd)
    dst = (SEG_AQ, SEG_AK, SEG_AV, SEG_IQ, None, None, SEG_BQ, SEG_BK, SEG_BV, SEG_G)
    for i, start in enumerate(dst):
        if start is not None:
            o_ref[0, :, start:start + widths[i]] = w_ref[0, :, src[i]:src[i + 1]].astype(BF16)
    ik = w_ref[0, :, src[4]:src[5]].astype(BF16)
    o_ref[0, :, SEG_IK:SEG_IK + HEAD_DIM] = ik
    o_ref[0, :, SEG_IK + HEAD_DIM:SEG_IK + LANES] = ik
    rows = o_ref.shape[1]
    o_ref[0, :, SEG_IW:SEG_IW + LANES] = jnp.zeros((rows, LANES), BF16)
    o_ref[0, :, SEG_IW:SEG_IW + IDX_HEADS] = w_ref[0, :, src[5]:src[6]].astype(BF16)


def _pack_w_in(w_in):
    depth, d, d_in = w_in.shape
    dp = SEG_G + 2 * d
    rows = 128 if d % 128 == 0 else d
    return pl.pallas_call(
        functools.partial(_pack_kernel, d_model=d),
        grid=(depth, d // rows),
        in_specs=[pl.BlockSpec((1, rows, d_in), lambda l, i: (l, i, 0))],
        out_specs=pl.BlockSpec((1, rows, dp), lambda l, i: (l, i, 0)),
        out_shape=jax.ShapeDtypeStruct((depth, d, dp), BF16),
        compiler_params=_params("parallel", "parallel"),
        name="pack_w_in",
    )(w_in)


def _in_projection(x2, cos, sin, attn_norm, wp, a_q_norm, a_k_norm, idx_k_norm, b_q_norm,
                   b_k_norm, tm, tq):
    n, d = x2.shape
    dp = wp.shape[1]
    blk = 2 * LANES
    r = jnp.arange(blk)
    bd = (r[:, None] // HEAD_DIM == r[None, :] // HEAD_DIM).astype(BF16)

    def tiled(g, reps):
        return jnp.tile(g.astype(F32), reps)[None, :]

    row = lambda w: pl.BlockSpec((tm, w), lambda i: (i, 0))
    out_w = [(D_A, BF16), (D_A, BF16), None, (IDX_W, F32), (LANES, F32), (LANES, F32),
             (D_B, BF16), (D_B, BF16), None, (2 * d, BF16)]
    vt_spec = pl.BlockSpec((tm // tq, D_A, tq), lambda i: (i, 0, 0))
    vt_shape = jax.ShapeDtypeStruct((n // tq, D_A, tq), BF16)
    return pl.pallas_call(
        _inproj_kernel,
        grid=(n // tm,),
        in_specs=[row(d), _const_spec((1, d)), _const_spec((d, dp)), _const_spec((blk, blk)),
                  _const_spec((1, D_A)), _const_spec((1, D_A)), _const_spec((1, LANES)),
                  _const_spec((1, D_B)), _const_spec((1, D_B)), row(LANES), row(LANES)],
        out_specs=[vt_spec if o is None else row(o[0]) for o in out_w],
        out_shape=[vt_shape if o is None else jax.ShapeDtypeStruct((n, o[0]), o[1]) for o in out_w],
        compiler_params=_params("parallel"),
        name="in_projection",
    )(x2, attn_norm.astype(F32)[None, :], wp, bd,
      tiled(a_q_norm, A_HEADS), tiled(a_k_norm, A_HEADS), tiled(idx_k_norm, 2),
      tiled(b_q_norm, 2 * B_HEADS), tiled(b_k_norm, 2 * B_HEADS), cos, sin)


def _dot_t(a, b):
    return lax.dot_general(a, b, (((1,), (1,)), ((), ())), preferred_element_type=F32)


def _logits(q, k, bias):
    s = _dot_t(k, q)
    return s if bias is None else s + bias


def _softmax_update(state, s, vt):
    m, acc = state
    m_new = jnp.maximum(m, jnp.max(s, axis=0, keepdims=True))
    alpha = jnp.exp2(m - m_new)
    p = jnp.exp2(s - m_new).astype(BF16)
    vt1 = jnp.concatenate([vt, jnp.ones((ONES_ROWS, vt.shape[1]), BF16)], axis=0)
    acc = alpha * acc + jnp.dot(vt1, p, preferred_element_type=F32)
    return m_new, acc


def _softmax_result(state, features):
    _, acc = state
    return acc[:features] / acc[features:features + 1]


def _advance_chains(states, steps):
    states = list(states)
    n = len(steps)
    logits = [None] * n
    for i in range(min(AHEAD, n)):
        logits[i] = steps[i][1]()
    for i in range(n):
        if i + AHEAD < n:
            logits[i + AHEAD] = steps[i + AHEAD][1]()
        chain, _, values_fn = steps[i]
        states[chain] = _softmax_update(states[chain], logits[i], values_fn())
        logits[i] = None
    return tuple(states)


def _stage_head_queries(q_ref, qm_scr, low_half):
    for hh in range(qm_scr.shape[0]):
        q2 = q_ref[:, (hh // 2) * LANES:(hh // 2 + 1) * LANES]
        own = low_half if hh % 2 == 0 else jnp.logical_not(low_half)
        qm_scr[hh] = jnp.where(own, q2, jnp.zeros_like(q2))


def _softmax_init(features, tq):
    return jnp.full((1, tq), MASKED, F32), jnp.zeros((features + ONES_ROWS, tq), F32)


def _sparse_kernel(iq_ref, iw_ref, ik_ref, aq_ref, ak_ref, avt_ref, o_ref,
                   key_scr, tie_scr, bias_scr, qm_scr, *, k_sel, tq):
    j = pl.program_id(1)
    n_chunks = j + 1
    nq = key_scr.shape[0]
    lane = lax.broadcasted_iota(I32, (1, LANES), 1)
    low_half = lane < HEAD_DIM
    q_pos = j * tq + lax.broadcasted_iota(I32, (1, tq), 1)
    key_iota = lax.broadcasted_iota(I32, (tq, 1), 0)

    w_t = iw_ref[...].T
    iq_heads = []
    for hh in range(IDX_HEADS):
        pair = iq_ref[:, (hh // 2) * LANES:(hh // 2 + 1) * LANES]
        own = low_half if hh % 2 == 0 else jnp.logical_not(low_half)
        iq_heads.append(jnp.where(own, pair, 0.0).astype(BF16))

    def score_chunk(c, carry):
        start = pl.multiple_of(c * tq, tq)
        ik = ik_ref[pl.ds(start, tq), :].astype(BF16)
        score = jnp.zeros((tq, tq), F32)
        for hh in range(IDX_HEADS):
            score = score + w_t[hh:hh + 1, :] * jnp.maximum(_dot_t(ik, iq_heads[hh]), 0.0)
        score = jnp.where(score == 0.0, 0.0, score)
        score = jnp.where(c * tq + key_iota <= q_pos, score, -jnp.inf)
        bits = lax.bitcast_convert_type(score, I32)
        key_scr[c] = bits ^ (lax.shift_right_arithmetic(bits, 31) & jnp.int32(0x7FFFFFFF))
        return carry

    lax.fori_loop(0, n_chunks, score_chunk, 0)

    def count(scr, pred):
        def body(c, acc):
            parts = [acc]
            for r in range(tq // SUBLANES):
                rows = scr[c, r * SUBLANES:(r + 1) * SUBLANES, :]
                parts.append(jnp.where(pred(rows), 1.0, 0.0))
            while len(parts) > 1:
                parts = [parts[i] + parts[i + 1] if i + 1 < len(parts) else parts[i]
                         for i in range(0, len(parts), 2)]
            return parts[0]
        acc = lax.fori_loop(0, n_chunks, body, jnp.zeros((SUBLANES, tq), F32))
        return jnp.sum(acc, axis=0, keepdims=True)

    def key_bit(i, carry):
        t_u, c_ge = carry
        cand_u = t_u | lax.shift_left(jnp.int32(1), 31 - i)
        cand = cand_u ^ jnp.int32(INT_MIN)
        cnt = count(key_scr, lambda kk: kk >= cand)
        ok = cnt >= k_sel
        return jnp.where(ok, cand_u, t_u), jnp.where(ok, cnt, c_ge)

    all_keys = jnp.zeros((1, tq), F32) + (n_chunks * tq).astype(F32)
    t_u, c_ge = lax.fori_loop(0, 32, key_bit, (jnp.zeros((1, tq), I32), all_keys))
    tau = t_u ^ jnp.int32(INT_MIN)
    n_keys = nq * tq
    tie_break = jnp.max(c_ge) > k_sel
    bias_scr[jnp.minimum(j + 1, nq - 1)] = jnp.full((tq, tq), MASKED, F32)

    @pl.when(jnp.logical_not(tie_break))
    def _():
        def bias_chunk(c, carry):
            sel = (key_scr[c] >= tau) & (c * tq + key_iota <= q_pos)
            bias_scr[c] = jnp.where(sel, 0.0, MASKED)
            return carry
        lax.fori_loop(0, n_chunks, bias_chunk, 0)

    @pl.when(tie_break)
    def _():
        need = k_sel - count(key_scr, lambda kk: kk > tau)

        def tie_chunk(c, carry):
            tie_scr[c] = jnp.where(key_scr[c] == tau, c * tq + key_iota, n_keys)
            return carry
        lax.fori_loop(0, n_chunks, tie_chunk, 0)
        idx_bits = max(1, (n_keys - 1).bit_length())

        def idx_bit(i, p):
            cand = p | lax.shift_left(jnp.int32(1), idx_bits - 1 - i)
            cnt = count(tie_scr, lambda tt: tt < cand)
            return jnp.where(cnt < need, cand, p)
        p_tie = lax.fori_loop(0, idx_bits, idx_bit, jnp.zeros((1, tq), I32))

        def bias_chunk(c, carry):
            sel = (key_scr[c] > tau) | (tie_scr[c] <= p_tie)
            sel = sel & (c * tq + key_iota <= q_pos)
            bias_scr[c] = jnp.where(sel, 0.0, MASKED)
            return carry
        lax.fori_loop(0, n_chunks, bias_chunk, 0)

    def chunk_steps(c):
        start = pl.multiple_of(c * tq, tq)
        steps = []
        for pr in range(A_HEADS // 2):
            cols = slice(pr * LANES, (pr + 1) * LANES)
            for half in range(2):
                hh = 2 * pr + half

                def chain_logits(cols=cols, hh=hh):
                    return _logits(qm_scr[hh], ak_ref[pl.ds(start, tq), cols], bias_scr[c])

                def chain_values(hh=hh):
                    return avt_ref[c, hh * HEAD_DIM:(hh + 1) * HEAD_DIM, :]
                steps.append((hh, chain_logits, chain_values))
        return steps

    def attend(c2, states):
        return _advance_chains(states, chunk_steps(2 * c2) + chunk_steps(2 * c2 + 1))

    _stage_head_queries(aq_ref, qm_scr, low_half)
    states = lax.fori_loop(0, (j + 2) // 2, attend,
                           tuple(_softmax_init(HEAD_DIM, tq) for _ in range(A_HEADS)))
    for pr in range(A_HEADS // 2):
        out_t = jnp.concatenate([_softmax_result(states[2 * pr], HEAD_DIM),
                                 _softmax_result(states[2 * pr + 1], HEAD_DIM)], axis=0)
        o_ref[:, pr * LANES:(pr + 1) * LANES] = out_t.T.astype(BF16)


def _sparse_attention(iq, iw, ik, aq, ak, avt, batch, seq, tq):
    k_sel = min(TOPK_MAX, seq // 4)
    nq = seq // tq
    qrow = lambda w: pl.BlockSpec((tq, w), lambda b, j: (b * nq + j, 0))
    full = lambda w: pl.BlockSpec((seq, w), lambda b, j: (b, 0))
    return pl.pallas_call(
        functools.partial(_sparse_kernel, k_sel=k_sel, tq=tq),
        grid=(batch, nq),
        in_specs=[qrow(IDX_W), qrow(LANES), full(LANES), qrow(D_A), full(D_A),
                  pl.BlockSpec((nq, D_A, tq), lambda b, j: (b, 0, 0))],
        out_specs=qrow(D_A),
        out_shape=jax.ShapeDtypeStruct((batch * seq, D_A), BF16),
        scratch_shapes=[pltpu.VMEM((nq, tq, tq), I32), pltpu.VMEM((nq, tq, tq), I32),
                        pltpu.VMEM((nq, tq, tq), F32), pltpu.VMEM((A_HEADS, tq, LANES), BF16)],
        compiler_params=_params("parallel", "arbitrary"),
        name="sparse_mixer",
    )(iq, iw, ik, aq, ak, avt)


def _diff_kernel(dl_ref, sub_ref, bq_ref, bk_ref, bvt_ref, o_ref, qm_scr, *, tq, lam_init):
    j = pl.program_id(1)
    lane = lax.broadcasted_iota(I32, (1, LANES), 1)
    low_half = lane < HEAD_DIM
    vdim = 2 * HEAD_DIM
    q_pos = j * tq + lax.broadcasted_iota(I32, (1, tq), 1)
    key_iota = lax.broadcasted_iota(I32, (tq, 1), 0)

    def chunk_steps(c, causal):
        start = pl.multiple_of(c * tq, tq)
        bias = jnp.where(c * tq + key_iota <= q_pos, 0.0, MASKED) if causal else None
        steps = []
        for hh in range(B_HEADS):
            cols = slice(hh * LANES, (hh + 1) * LANES)
            for cc in range(2):
                def chain_logits(cols=cols, ch=2 * hh + cc):
                    return _logits(qm_scr[ch], bk_ref[pl.ds(start, tq), cols], bias)

                def chain_values(hh=hh):
                    return bvt_ref[c, hh * vdim:(hh + 1) * vdim, :]
                steps.append((2 * hh + cc, chain_logits, chain_values))
        return steps

    def pair(c2, states, causal):
        return _advance_chains(states,
                               chunk_steps(2 * c2, causal) + chunk_steps(2 * c2 + 1, causal))

    last = (j + 2) // 2 - 1
    _stage_head_queries(bq_ref, qm_scr, low_half)
    init = tuple(_softmax_init(vdim, tq) for _ in range(2 * B_HEADS))
    states = lax.fori_loop(0, last, lambda c2, st: pair(c2, st, False), init)
    states = pair(last, states, True)

    dl = dl_ref[...]
    lam = (jnp.exp(jnp.sum(dl[0:1] * dl[1:2], axis=1, keepdims=True))
           - jnp.exp(jnp.sum(dl[2:3] * dl[3:4], axis=1, keepdims=True)) + lam_init)
    for hh in range(B_HEADS):
        y = (_softmax_result(states[2 * hh], vdim)
             - lam * _softmax_result(states[2 * hh + 1], vdim))
        ms = jnp.mean(y * y, axis=0, keepdims=True)
        y = y * lax.rsqrt(ms + EPS) * sub_ref[...] * (1.0 - lam_init)
        o_ref[:, hh * LANES:(hh + 1) * LANES] = y.T.astype(BF16)


def _diff_attention(diff_lambda, b_subln, bq, bk, bvt, batch, seq, tq, layer_idx):
    nq = seq // tq
    lam_init = 0.8 - 0.6 * math.exp(-0.3 * layer_idx)
    subln = jnp.broadcast_to(b_subln.astype(F32)[:, None], (2 * HEAD_DIM, tq))
    qrow = lambda w: pl.BlockSpec((tq, w), lambda b, j: (b * nq + j, 0))
    full = lambda w: pl.BlockSpec((seq, w), lambda b, j: (b, 0))
    return pl.pallas_call(
        functools.partial(_diff_kernel, tq=tq, lam_init=lam_init),
        grid=(batch, nq),
        in_specs=[_const_spec((4, HEAD_DIM)), _const_spec((2 * HEAD_DIM, tq)), qrow(D_B),
                  full(D_B), pl.BlockSpec((nq, D_B, tq), lambda b, j: (b, 0, 0))],
        out_specs=qrow(D_B),
        out_shape=jax.ShapeDtypeStruct((batch * seq, D_B), BF16),
        scratch_shapes=[pltpu.VMEM((2 * B_HEADS, tq, LANES), BF16)],
        compiler_params=_params("parallel", "arbitrary"),
        name="diff_mixer",
    )(diff_lambda.astype(F32), subln, bq, bk, bvt)


def _sigmoid(z):
    return 1.0 / (1.0 + jnp.exp(-z))


def _out_ffn_kernel(x_ref, ya_ref, yb_ref, g_ref, gb_ref, wua_ref, wub_ref, wo_ref, fn_ref,
                    wg_ref, wu_ref, wd_ref, o_ref, *, fc):
    d = x_ref.shape[1]
    ua = jnp.dot(ya_ref[...], wua_ref[...], preferred_element_type=F32)
    ub = jnp.dot(yb_ref[...], wub_ref[...], preferred_element_type=F32)
    g = _sigmoid(g_ref[...].astype(F32) + gb_ref[...])
    merged = g[:, :d] * ua + g[:, d:] * ub
    x1 = x_ref[...] + jnp.dot(merged.astype(BF16), wo_ref[...], preferred_element_type=F32)
    ms = jnp.mean(x1 * x1, axis=-1, keepdims=True)
    h = (x1 * lax.rsqrt(ms + EPS) * fn_ref[...]).astype(BF16)
    acc = x1
    for c in range(wd_ref.shape[0] // fc):
        cols = slice(c * fc, (c + 1) * fc)
        gate = jnp.dot(h, wg_ref[:, cols], preferred_element_type=F32)
        up = jnp.dot(h, wu_ref[:, cols], preferred_element_type=F32)
        act = (gate * _sigmoid(gate) * up).astype(BF16)
        acc = acc + jnp.dot(act, wd_ref[cols, :], preferred_element_type=F32)
    o_ref[...] = acc


def _out_ffn(x2, ya, yb, gates, gate_bias, w_up_a, w_up_b, w_out, ffn_norm, w_ffn_in, w_ffn_out,
             tm, fc):
    n, d = x2.shape
    hidden = w_ffn_out.shape[0]
    wg = w_ffn_in[:, :hidden].astype(BF16)
    wu = w_ffn_in[:, hidden:].astype(BF16)
    row = lambda w: pl.BlockSpec((tm, w), lambda i: (i, 0))
    return pl.pallas_call(
        functools.partial(_out_ffn_kernel, fc=fc),
        grid=(n // tm,),
        in_specs=[row(d), row(D_A), row(D_B), row(2 * d), _const_spec((1, 2 * d)),
                  _const_spec((D_A, d)), _const_spec((D_B, d)), _const_spec((d, d)),
                  _const_spec((1, d)), _const_spec((d, hidden)), _const_spec((d, hidden)),
                  _const_spec((hidden, d))],
        out_specs=row(d),
        out_shape=jax.ShapeDtypeStruct((n, d), F32),
        compiler_params=_params("parallel"),
        name="out_ffn",
    )(x2, ya, yb, gates, gate_bias.astype(F32)[None, :], w_up_a.astype(BF16),
      w_up_b.astype(BF16), w_out.astype(BF16), ffn_norm.astype(F32)[None, :], wg, wu,
      w_ffn_out.astype(BF16))


def _tiles(batch, seq, hidden):
    n = batch * seq
    tm = 512 if n % 512 == 0 else seq
    tq = 256
    assert seq % (2 * tq) == 0 and n % tm == 0 and tm % tq == 0
    fc = 256 if hidden % 256 == 0 else hidden
    return tm, tq, fc


def kernel(x, positions, attn_norm, w_in, gate_bias, a_q_norm, a_k_norm, idx_k_norm, b_q_norm, b_k_norm, diff_lambda, b_subln, w_up_a, w_up_b, w_out, ffn_norm, w_ffn_in, w_ffn_out):
    batch, seq, d = x.shape
    depth = w_in.shape[0]
    tm, tq, fc = _tiles(batch, seq, w_ffn_out.shape[1])
    cos, sin = _rope_tables(positions)
    w_in_packed = _pack_w_in(w_in)
    x2 = x.reshape(batch * seq, d)
    for l in range(depth):
        aq, ak, avt, iq, ik, iw, bq, bk, bvt, gates = _in_projection(
            x2, cos, sin, attn_norm[l], w_in_packed[l], a_q_norm[l], a_k_norm[l], idx_k_norm[l],
            b_q_norm[l], b_k_norm[l], tm, tq)
        ya = _sparse_attention(iq, iw, ik, aq, ak, avt, batch, seq, tq)
        yb = _diff_attention(diff_lambda[l], b_subln[l], bq, bk, bvt, batch, seq, tq, l)
        x2 = _out_ffn(x2, ya, yb, gates, gate_bias[l], w_up_a[l], w_up_b[l], w_out[l],
                      ffn_norm[l], w_ffn_in[l], w_ffn_out[l], tm, fc)
    return x2.reshape(batch, seq, d)
```

```python
import functools
import math

import jax
import jax.numpy as jnp
from jax import lax
from jax.experimental import pallas as pl
from jax.experimental.pallas import tpu as pltpu

F32 = jnp.float32
BF16 = jnp.bfloat16
I32 = jnp.int32
I16 = jnp.int16

HEAD_DIM = 64
HALF = HEAD_DIM // 2
A_HEADS = 8
IDX_HEADS = 4
B_HEADS = 4
TOPK_MAX = 256
ROPE_THETA = 10000.0
EPS = 1e-6
LANES = 128
SUBLANES = 8
MASKED = -1e30
INT_MIN = -(2 ** 31)
LOG2E = math.log2(math.e)
VMEM_LIMIT = 56 * 1024 * 1024
ONES_ROWS = 16
AHEAD = 6

D_A = A_HEADS * HEAD_DIM
D_B = B_HEADS * 2 * HEAD_DIM
IDX_W = IDX_HEADS * HEAD_DIM

SEG_AQ = 0
SEG_AK = SEG_AQ + D_A
SEG_AV = SEG_AK + D_A
SEG_IQ = SEG_AV + D_A
SEG_IK = SEG_IQ + IDX_W
SEG_IW = SEG_IK + LANES
SEG_BQ = SEG_IW + LANES
SEG_BK = SEG_BQ + D_B
SEG_BV = SEG_BK + D_B
SEG_G = SEG_BV + D_B


def _const_spec(shape):
    nd = len(shape)
    return pl.BlockSpec(shape, lambda *_: (0,) * nd, pipeline_mode=pl.Buffered(1))


def _params(*sem):
    return pltpu.CompilerParams(dimension_semantics=sem, vmem_limit_bytes=VMEM_LIMIT)


def _trig_kernel(pos_ref, invf_ref, sign_ref, cos_ref, sin_ref):
    ang = pos_ref[...].astype(F32) * invf_ref[...]
    cos_ref[...] = jnp.cos(ang)
    sin_ref[...] = jnp.sin(ang) * sign_ref[...]


def _rope_tables(positions):
    n = positions.size
    inv_freq = 1.0 / (ROPE_THETA ** (jnp.arange(0, HEAD_DIM, 2, dtype=F32) / HEAD_DIM))
    invf = jnp.tile(inv_freq, LANES // HALF)[None, :]
    sign = jnp.tile(jnp.concatenate([-jnp.ones(HALF, F32), jnp.ones(HALF, F32)]),
                    LANES // HEAD_DIM)[None, :]
    tr = min(n, 2048)
    return pl.pallas_call(
        _trig_kernel,
        grid=(n // tr,),
        in_specs=[pl.BlockSpec((tr, 1), lambda i: (i, 0)), _const_spec((1, LANES)),
                  _const_spec((1, LANES))],
        out_specs=[pl.BlockSpec((tr, LANES), lambda i: (i, 0))] * 2,
        out_shape=[jax.ShapeDtypeStruct((n, LANES), F32)] * 2,
        compiler_params=_params("parallel"),
        name="rope_trig",
    )(positions.reshape(n, 1), invf, sign)


def _rope(y, cos, sin, first_half):
    cols = []
    for c in range(y.shape[1] // LANES):
        t = y[:, c * LANES:(c + 1) * LANES]
        partner = jnp.where(first_half, pltpu.roll(t, LANES - HALF, 1), pltpu.roll(t, HALF, 1))
        cols.append(t * cos + partner * sin)
    return cols[0] if len(cols) == 1 else jnp.concatenate(cols, axis=1)


def _head_sumsq(y, bd):
    sq = (y * y).astype(BF16)
    w = min(bd.shape[0], y.shape[1])
    bd = bd[:w, :w]
    cols = [jnp.dot(sq[:, c * w:(c + 1) * w], bd, preferred_element_type=F32)
            for c in range(y.shape[1] // w)]
    return cols[0] if len(cols) == 1 else jnp.concatenate(cols, axis=1)


def _inproj_kernel(x_ref, gn_ref, w_ref, bd_ref, gaq_ref, gak_ref, gik_ref, gbq_ref, gbk_ref,
                   cos_ref, sin_ref,
                   aq_ref, ak_ref, avt_ref, iq_ref, ik_ref, iw_ref, bq_ref, bk_ref, bvt_ref, g_ref):
    x = x_ref[...]
    ms = jnp.mean(x * x, axis=-1, keepdims=True)
    h = (x * lax.rsqrt(ms + EPS) * gn_ref[...]).astype(BF16)
    cos = cos_ref[...]
    sin = sin_ref[...]
    bd = bd_ref[...]
    lane = lax.broadcasted_iota(I32, (1, LANES), 1)
    first_half = (lane % HEAD_DIM) < HALF
    scale = HEAD_DIM ** -0.5

    def proj(start, width):
        return jnp.dot(h, w_ref[:, start:start + width], preferred_element_type=F32)

    def store_transposed(out_ref, y):
        tq = out_ref.shape[2]
        for s in range(out_ref.shape[0]):
            out_ref[s] = y[s * tq:(s + 1) * tq, :].T.astype(BF16)

    def normed_rope(start, width, gain_ref, out_scale):
        y = proj(start, width)
        y = y * lax.rsqrt(_head_sumsq(y, bd) * (1.0 / HEAD_DIM) + EPS) * gain_ref[...]
        y = _rope(y, cos, sin, first_half)
        return y if out_scale is None else y * out_scale

    aq_ref[...] = normed_rope(SEG_AQ, D_A, gaq_ref, scale * LOG2E).astype(BF16)
    ak_ref[...] = normed_rope(SEG_AK, D_A, gak_ref, None).astype(BF16)
    store_transposed(avt_ref, proj(SEG_AV, D_A))
    iq_ref[...] = _rope(proj(SEG_IQ, IDX_W), cos, sin, first_half) * scale
    ik_ref[...] = normed_rope(SEG_IK, LANES, gik_ref, None)
    iw_ref[...] = proj(SEG_IW, LANES) * (IDX_HEADS ** -0.5)
    bq_ref[...] = normed_rope(SEG_BQ, D_B, gbq_ref, scale * LOG2E).astype(BF16)
    bk_ref[...] = normed_rope(SEG_BK, D_B, gbk_ref, None).astype(BF16)
    store_transposed(bvt_ref, proj(SEG_BV, D_B))
    g_ref[...] = proj(SEG_G, g_ref.shape[1]).astype(BF16)


def _pack_kernel(w_ref, o_ref, *, d_model):
    widths = (D_A, D_A, D_A, IDX_W, HEAD_DIM, IDX_HEADS, D_B, D_B, D_B, 2 * d_model)
    src = [0]
    for wd in widths:
        src.append(src[-1] + wd)
    dst = (SEG_AQ, SEG_AK, SEG_AV, SEG_IQ, None, None, SEG_BQ, SEG_BK, SEG_BV, SEG_G)
    for i, start in enumerate(dst):
        if start is not None:
            o_ref[0, :, start:start + widths[i]] = w_ref[0, :, src[i]:src[i + 1]].astype(BF16)
    ik = w_ref[0, :, src[4]:src[5]].astype(BF16)
    o_ref[0, :, SEG_IK:SEG_IK + HEAD_DIM] = ik
    o_ref[0, :, SEG_IK + HEAD_DIM:SEG_IK + LANES] = ik
    rows = o_ref.shape[1]
    o_ref[0, :, SEG_IW:SEG_IW + LANES] = jnp.zeros((rows, LANES), BF16)
    o_ref[0, :, SEG_IW:SEG_IW + IDX_HEADS] = w_ref[0, :, src[5]:src[6]].astype(BF16)


def _pack_w_in(w_in):
    depth, d, d_in = w_in.shape
    dp = SEG_G + 2 * d
    rows = 128 if d % 128 == 0 else d
    return pl.pallas_call(
        functools.partial(_pack_kernel, d_model=d),
        grid=(depth, d // rows),
        in_specs=[pl.BlockSpec((1, rows, d_in), lambda l, i: (l, i, 0))],
        out_specs=pl.BlockSpec((1, rows, dp), lambda l, i: (l, i, 0)),
        out_shape=jax.ShapeDtypeStruct((depth, d, dp), BF16),
        compiler_params=_params("parallel", "parallel"),
        name="pack_w_in",
    )(w_in)


def _in_projection(x2, cos, sin, attn_norm, wp, a_q_norm, a_k_norm, idx_k_norm, b_q_norm,
                   b_k_norm, tm, tq):
    n, d = x2.shape
    dp = wp.shape[1]
    blk = 2 * LANES
    r = jnp.arange(blk)
    bd = (r[:, None] // HEAD_DIM == r[None, :] // HEAD_DIM).astype(BF16)

    def tiled(g, reps):
        return jnp.tile(g.astype(F32), reps)[None, :]

    row = lambda w: pl.BlockSpec((tm, w), lambda i: (i, 0))
    out_w = [(D_A, BF16), (D_A, BF16), None, (IDX_W, F32), (LANES, F32), (LANES, F32),
             (D_B, BF16), (D_B, BF16), None, (2 * d, BF16)]
    vt_spec = pl.BlockSpec((tm // tq, D_A, tq), lambda i: (i, 0, 0))
    vt_shape = jax.ShapeDtypeStruct((n // tq, D_A, tq), BF16)
    return pl.pallas_call(
        _inproj_kernel,
        grid=(n // tm,),
        in_specs=[row(d), _const_spec((1, d)), _const_spec((d, dp)), _const_spec((blk, blk)),
                  _const_spec((1, D_A)), _const_spec((1, D_A)), _const_spec((1, LANES)),
                  _const_spec((1, D_B)), _const_spec((1, D_B)), row(LANES), row(LANES)],
        out_specs=[vt_spec if o is None else row(o[0]) for o in out_w],
        out_shape=[vt_shape if o is None else jax.ShapeDtypeStruct((n, o[0]), o[1]) for o in out_w],
        compiler_params=_params("parallel"),
        name="in_projection",
    )(x2, attn_norm.astype(F32)[None, :], wp, bd,
      tiled(a_q_norm, A_HEADS), tiled(a_k_norm, A_HEADS), tiled(idx_k_norm, 2),
      tiled(b_q_norm, 2 * B_HEADS), tiled(b_k_norm, 2 * B_HEADS), cos, sin)


def _dot_t(a, b):
    return lax.dot_general(a, b, (((1,), (1,)), ((), ())), preferred_element_type=F32)


def _logits(q, k, bias):
    s = _dot_t(k, q)
    return s if bias is None else s + bias


def _softmax_update(state, s, vt):
    m, acc = state
    m_new = jnp.maximum(m, jnp.max(s, axis=0, keepdims=True))
    alpha = jnp.exp2(m - m_new)
    p = jnp.exp2(s - m_new).astype(BF16)
    vt1 = jnp.concatenate([vt, jnp.ones((ONES_ROWS, vt.shape[1]), BF16)], axis=0)
    acc = alpha * acc + jnp.dot(vt1, p, preferred_element_type=F32)
    return m_new, acc


def _softmax_result(state, features):
    _, acc = state
    return acc[:features] / acc[features:features + 1]


def _advance_chains(states, steps):
    states = list(states)
    n = len(steps)
    logits = [None] * n
    for i in range(min(AHEAD, n)):
        logits[i] = steps[i][1]()
    for i in range(n):
        if i + AHEAD < n:
            logits[i + AHEAD] = steps[i + AHEAD][1]()
        chain, _, values_fn = steps[i]
        states[chain] = _softmax_update(states[chain], logits[i], values_fn())
        logits[i] = None
    return tuple(states)


def _stage_head_queries(q_ref, qm_scr, low_half):
    for hh in range(qm_scr.shape[0]):
        q2 = q_ref[:, (hh // 2) * LANES:(hh // 2 + 1) * LANES]
        own = low_half if hh % 2 == 0 else jnp.logical_not(low_half)
        qm_scr[hh] = jnp.where(own, q2, jnp.zeros_like(q2))


def _softmax_init(features, tq):
    return jnp.full((1, tq), MASKED, F32), jnp.zeros((features + ONES_ROWS, tq), F32)


def _sparse_kernel(iq_ref, iw_ref, ik_ref, aq_ref, ak_ref, avt_ref, o_ref,
                   key_scr, tie_scr, bias_scr, qm_scr, hi_scr, lo_scr, *, k_sel, tq):
    j = pl.program_id(1)
    n_chunks = j + 1
    nq = key_scr.shape[0]
    lane = lax.broadcasted_iota(I32, (1, LANES), 1)
    low_half = lane < HEAD_DIM
    q_pos = j * tq + lax.broadcasted_iota(I32, (1, tq), 1)
    key_iota = lax.broadcasted_iota(I32, (tq, 1), 0)

    w_t = iw_ref[...].T
    iq_heads = []
    for hh in range(IDX_HEADS):
        pair = iq_ref[:, (hh // 2) * LANES:(hh // 2 + 1) * LANES]
        own = low_half if hh % 2 == 0 else jnp.logical_not(low_half)
        iq_heads.append(jnp.where(own, pair, 0.0).astype(BF16))

    def score_chunk(c, carry):
        start = pl.multiple_of(c * tq, tq)
        ik = ik_ref[pl.ds(start, tq), :].astype(BF16)
        score = jnp.zeros((tq, tq), F32)
        for hh in range(IDX_HEADS):
            score = score + w_t[hh:hh + 1, :] * jnp.maximum(_dot_t(ik, iq_heads[hh]), 0.0)
        score = jnp.where(score == 0.0, 0.0, score)
        score = jnp.where(c * tq + key_iota <= q_pos, score, -jnp.inf)
        bits = lax.bitcast_convert_type(score, I32)
        key = bits ^ (lax.shift_right_arithmetic(bits, 31) & jnp.int32(0x7FFFFFFF))
        key_scr[c] = key
        hi_scr[c] = lax.shift_right_arithmetic(key, 16).astype(I16)
        return carry

    lax.fori_loop(0, n_chunks, score_chunk, 0)

    def count(scr, pred):
        rows_per_vreg = SUBLANES * 4 // scr.dtype.itemsize
        acc_dtype = F32 if scr.dtype.itemsize == 4 else I16
        one = jnp.ones((rows_per_vreg, tq), acc_dtype)
        zero = jnp.zeros((rows_per_vreg, tq), acc_dtype)

        def body(c, acc):
            parts = [acc]
            for r in range(tq // rows_per_vreg):
                rows = scr[c, r * rows_per_vreg:(r + 1) * rows_per_vreg, :]
                parts.append(jnp.where(pred(rows), one, zero))
            while len(parts) > 1:
                parts = [parts[i] + parts[i + 1] if i + 1 < len(parts) else parts[i]
                         for i in range(0, len(parts), 2)]
            return parts[0]
        acc = lax.fori_loop(0, n_chunks, body, zero)
        return jnp.sum(acc.astype(F32), axis=0, keepdims=True)

    def accept(carry, cand_u, cnt):
        t_u, c_ge = carry
        ok = cnt >= k_sel
        return jnp.where(ok, cand_u, t_u), jnp.where(ok, cnt, c_ge)

    def hi_bit(i, carry):
        cand_u = carry[0] | lax.shift_left(jnp.int32(1), 31 - i)
        cand_hi = lax.shift_right_arithmetic(cand_u ^ jnp.int32(INT_MIN), 16).astype(I16)
        return accept(carry, cand_u, count(hi_scr, lambda hh: hh >= cand_hi))

    all_keys = jnp.zeros((1, tq), F32) + (n_chunks * tq).astype(F32)
    carry = lax.fori_loop(0, 16, hi_bit, (jnp.zeros((1, tq), I32), all_keys))

    hi_sel = lax.shift_right_arithmetic(carry[0] ^ jnp.int32(INT_MIN), 16)
    hi_sel16 = hi_sel.astype(I16)
    c_above = count(hi_scr, lambda hh: hh > hi_sel16)
    low_min = -(2 ** 15)

    def lo_chunk(c, carry):
        key = key_scr[c]
        low = (key & jnp.int32(0xFFFF)) + low_min
        same_hi = lax.shift_right_arithmetic(key, 16) == hi_sel
        lo_scr[c] = jnp.where(same_hi, low, low_min).astype(I16)
        return carry

    lax.fori_loop(0, n_chunks, lo_chunk, 0)

    def lo_bit(i, carry):
        cand_u = carry[0] | lax.shift_left(jnp.int32(1), 31 - i)
        cand_lo = ((cand_u & jnp.int32(0xFFFF)) + low_min).astype(I16)
        return accept(carry, cand_u, c_above + count(lo_scr, lambda ll: ll >= cand_lo))

    t_u, c_ge = lax.fori_loop(16, 32, lo_bit, carry)
    tau = t_u ^ jnp.int32(INT_MIN)
    n_keys = nq * tq
    tie_break = jnp.max(c_ge) > k_sel
    bias_scr[jnp.minimum(j + 1, nq - 1)] = jnp.full((tq, tq), MASKED, F32)

    @pl.when(jnp.logical_not(tie_break))
    def _():
        def bias_chunk(c, carry):
            sel = (key_scr[c] >= tau) & (c * tq + key_iota <= q_pos)
            bias_scr[c] = jnp.where(sel, 0.0, MASKED)
            return carry
        lax.fori_loop(0, n_chunks, bias_chunk, 0)

    @pl.when(tie_break)
    def _():
        need = k_sel - count(key_scr, lambda kk: kk > tau)

        def tie_chunk(c, carry):
            tie_scr[c] = jnp.where(key_scr[c] == tau, c * tq + key_iota, n_keys)
            return carry
        lax.fori_loop(0, n_chunks, tie_chunk, 0)
        idx_bits = max(1, (n_keys - 1).bit_length())

        def idx_bit(i, p):
            cand = p | lax.shift_left(jnp.int32(1), idx_bits - 1 - i)
            cnt = count(tie_scr, lambda tt: tt < cand)
            return jnp.where(cnt < need, cand, p)
        p_tie = lax.fori_loop(0, idx_bits, idx_bit, jnp.zeros((1, tq), I32))

        def bias_chunk(c, carry):
            sel = (key_scr[c] > tau) | (tie_scr[c] <= p_tie)
            sel = sel & (c * tq + key_iota <= q_pos)
            bias_scr[c] = jnp.where(sel, 0.0, MASKED)
            return carry
        lax.fori_loop(0, n_chunks, bias_chunk, 0)

    def chunk_steps(c):
        start = pl.multiple_of(c * tq, tq)
        steps = []
        for pr in range(A_HEADS // 2):
            cols = slice(pr * LANES, (pr + 1) * LANES)
            for half in range(2):
                hh = 2 * pr + half

                def chain_logits(cols=cols, hh=hh):
                    return _logits(qm_scr[hh], ak_ref[pl.ds(start, tq), cols], bias_scr[c])

                def chain_values(hh=hh):
                    return avt_ref[c, hh * HEAD_DIM:(hh + 1) * HEAD_DIM, :]
                steps.append((hh, chain_logits, chain_values))
        return steps

    def attend(c2, states):
        return _advance_chains(states, chunk_steps(2 * c2) + chunk_steps(2 * c2 + 1))

    _stage_head_queries(aq_ref, qm_scr, low_half)
    states = lax.fori_loop(0, (j + 2) // 2, attend,
                           tuple(_softmax_init(HEAD_DIM, tq) for _ in range(A_HEADS)))
    for pr in range(A_HEADS // 2):
        out_t = jnp.concatenate([_softmax_result(states[2 * pr], HEAD_DIM),
                                 _softmax_result(states[2 * pr + 1], HEAD_DIM)], axis=0)
        o_ref[:, pr * LANES:(pr + 1) * LANES] = out_t.T.astype(BF16)


def _sparse_attention(iq, iw, ik, aq, ak, avt, batch, seq, tq):
    k_sel = min(TOPK_MAX, seq // 4)
    nq = seq // tq
    qrow = lambda w: pl.BlockSpec((tq, w), lambda b, j: (b * nq + j, 0))
    full = lambda w: pl.BlockSpec((seq, w), lambda b, j: (b, 0))
    return pl.pallas_call(
        functools.partial(_sparse_kernel, k_sel=k_sel, tq=tq),
        grid=(batch, nq),
        in_specs=[qrow(IDX_W), qrow(LANES), full(LANES), qrow(D_A), full(D_A),
                  pl.BlockSpec((nq, D_A, tq), lambda b, j: (b, 0, 0))],
        out_specs=qrow(D_A),
        out_shape=jax.ShapeDtypeStruct((batch * seq, D_A), BF16),
        scratch_shapes=[pltpu.VMEM((nq, tq, tq), I32), pltpu.VMEM((nq, tq, tq), I32),
                        pltpu.VMEM((nq, tq, tq), F32), pltpu.VMEM((A_HEADS, tq, LANES), BF16),
                        pltpu.VMEM((nq, tq, tq), I16), pltpu.VMEM((nq, tq, tq), I16)],
        compiler_params=_params("parallel", "arbitrary"),
        name="sparse_mixer",
    )(iq, iw, ik, aq, ak, avt)


def _diff_kernel(dl_ref, sub_ref, bq_ref, bk_ref, bvt_ref, o_ref, qm_scr, *, tq, lam_init):
    j = pl.program_id(1)
    lane = lax.broadcasted_iota(I32, (1, LANES), 1)
    low_half = lane < HEAD_DIM
    vdim = 2 * HEAD_DIM
    q_pos = j * tq + lax.broadcasted_iota(I32, (1, tq), 1)
    key_iota = lax.broadcasted_iota(I32, (tq, 1), 0)

    def chunk_steps(c, causal):
        start = pl.multiple_of(c * tq, tq)
        bias = jnp.where(c * tq + key_iota <= q_pos, 0.0, MASKED) if causal else None
        steps = []
        for hh in range(B_HEADS):
            cols = slice(hh * LANES, (hh + 1) * LANES)
            for cc in range(2):
                def chain_logits(cols=cols, ch=2 * hh + cc):
                    return _logits(qm_scr[ch], bk_ref[pl.ds(start, tq), cols], bias)

                def chain_values(hh=hh):
                    return bvt_ref[c, hh * vdim:(hh + 1) * vdim, :]
                steps.append((2 * hh + cc, chain_logits, chain_values))
        return steps

    def pair(c2, states, causal):
        return _advance_chains(states,
                               chunk_steps(2 * c2, causal) + chunk_steps(2 * c2 + 1, causal))

    last = (j + 2) // 2 - 1
    _stage_head_queries(bq_ref, qm_scr, low_half)
    init = tuple(_softmax_init(vdim, tq) for _ in range(2 * B_HEADS))
    states = lax.fori_loop(0, last, lambda c2, st: pair(c2, st, False), init)
    states = pair(last, states, True)

    dl = dl_ref[...]
    lam = (jnp.exp(jnp.sum(dl[0:1] * dl[1:2], axis=1, keepdims=True))
           - jnp.exp(jnp.sum(dl[2:3] * dl[3:4], axis=1, keepdims=True)) + lam_init)
    for hh in range(B_HEADS):
        y = (_softmax_result(states[2 * hh], vdim)
             - lam * _softmax_result(states[2 * hh + 1], vdim))
        ms = jnp.mean(y * y, axis=0, keepdims=True)
        y = y * lax.rsqrt(ms + EPS) * sub_ref[...] * (1.0 - lam_init)
        o_ref[:, hh * LANES:(hh + 1) * LANES] = y.T.astype(BF16)


def _diff_attention(diff_lambda, b_subln, bq, bk, bvt, batch, seq, tq, layer_idx):
    nq = seq // tq
    lam_init = 0.8 - 0.6 * math.exp(-0.3 * layer_idx)
    subln = jnp.broadcast_to(b_subln.astype(F32)[:, None], (2 * HEAD_DIM, tq))
    qrow = lambda w: pl.BlockSpec((tq, w), lambda b, j: (b * nq + j, 0))
    full = lambda w: pl.BlockSpec((seq, w), lambda b, j: (b, 0))
    return pl.pallas_call(
        functools.partial(_diff_kernel, tq=tq, lam_init=lam_init),
        grid=(batch, nq),
        in_specs=[_const_spec((4, HEAD_DIM)), _const_spec((2 * HEAD_DIM, tq)), qrow(D_B),
                  full(D_B), pl.BlockSpec((nq, D_B, tq), lambda b, j: (b, 0, 0))],
        out_specs=qrow(D_B),
        out_shape=jax.ShapeDtypeStruct((batch * seq, D_B), BF16),
        scratch_shapes=[pltpu.VMEM((2 * B_HEADS, tq, LANES), BF16)],
        compiler_params=_params("parallel", "arbitrary"),
        name="diff_mixer",
    )(diff_lambda.astype(F32), subln, bq, bk, bvt)


def _sigmoid(z):
    return 1.0 / (1.0 + jnp.exp(-z))


def _out_ffn_kernel(x_ref, ya_ref, yb_ref, g_ref, gb_ref, wua_ref, wub_ref, wo_ref, fn_ref,
                    wg_ref, wu_ref, wd_ref, o_ref, *, fc):
    d = x_ref.shape[1]
    ua = jnp.dot(ya_ref[...], wua_ref[...], preferred_element_type=F32)
    ub = jnp.dot(yb_ref[...], wub_ref[...], preferred_element_type=F32)
    g = _sigmoid(g_ref[...].astype(F32) + gb_ref[...])
    merged = g[:, :d] * ua + g[:, d:] * ub
    x1 = x_ref[...] + jnp.dot(merged.astype(BF16), wo_ref[...], preferred_element_type=F32)
    ms = jnp.mean(x1 * x1, axis=-1, keepdims=True)
    h = (x1 * lax.rsqrt(ms + EPS) * fn_ref[...]).astype(BF16)
    acc = x1
    for c in range(wd_ref.shape[0] // fc):
        cols = slice(c * fc, (c + 1) * fc)
        gate = jnp.dot(h, wg_ref[:, cols], preferred_element_type=F32)
        up = jnp.dot(h, wu_ref[:, cols], preferred_element_type=F32)
        act = (gate * _sigmoid(gate) * up).astype(BF16)
        acc = acc + jnp.dot(act, wd_ref[cols, :], preferred_element_type=F32)
    o_ref[...] = acc


def _out_ffn(x2, ya, yb, gates, gate_bias, w_up_a, w_up_b, w_out, ffn_norm, w_ffn_in, w_ffn_out,
             tm, fc):
    n, d = x2.shape
    hidden = w_ffn_out.shape[0]
    wg = w_ffn_in[:, :hidden].astype(BF16)
    wu = w_ffn_in[:, hidden:].astype(BF16)
    row = lambda w: pl.BlockSpec((tm, w), lambda i: (i, 0))
    return pl.pallas_call(
        functools.partial(_out_ffn_kernel, fc=fc),
        grid=(n // tm,),
        in_specs=[row(d), row(D_A), row(D_B), row(2 * d), _const_spec((1, 2 * d)),
                  _const_spec((D_A, d)), _const_spec((D_B, d)), _const_spec((d, d)),
                  _const_spec((1, d)), _const_spec((d, hidden)), _const_spec((d, hidden)),
                  _const_spec((hidden, d))],
        out_specs=row(d),
        out_shape=jax.ShapeDtypeStruct((n, d), F32),
        compiler_params=_params("parallel"),
        name="out_ffn",
    )(x2, ya, yb, gates, gate_bias.astype(F32)[None, :], w_up_a.astype(BF16),
      w_up_b.astype(BF16), w_out.astype(BF16), ffn_norm.astype(F32)[None, :], wg, wu,
      w_ffn_out.astype(BF16))


def _tiles(batch, seq, hidden):
    n = batch * seq
    tm = 512 if n % 512 == 0 else seq
    tq = 256
    assert seq % (2 * tq) == 0 and n % tm == 0 and tm % tq == 0
    fc = 256 if hidden % 256 == 0 else hidden
    return tm, tq, fc


def kernel(x, positions, attn_norm, w_in, gate_bias, a_q_norm, a_k_norm, idx_k_norm, b_q_norm, b_k_norm, diff_lambda, b_subln, w_up_a, w_up_b, w_out, ffn_norm, w_ffn_in, w_ffn_out):
    batch, seq, d = x.shape
    depth = w_in.shape[0]
    tm, tq, fc = _tiles(batch, seq, w_ffn_out.shape[1])
    cos, sin = _rope_tables(positions)
    w_in_packed = _pack_w_in(w_in)
    x2 = x.reshape(batch * seq, d)
    for l in range(depth):
        aq, ak, avt, iq, ik, iw, bq, bk, bvt, gates = _in_projection(
            x2, cos, sin, attn_norm[l], w_in_packed[l], a_q_norm[l], a_k_norm[l], idx_k_norm[l],
            b_q_norm[l], b_k_norm[l], tm, tq)
        ya = _sparse_attention(iq, iw, ik, aq, ak, avt, batch, seq, tq)
        yb = _diff_attention(diff_lambda[l], b_subln[l], bq, bk, bvt, batch, seq, tq, l)
        x2 = _out_ffn(x2, ya, yb, gates, gate_bias[l], w_up_a[l], w_up_b[l], w_out[l],
                      ffn_norm[l], w_ffn_in[l], w_ffn_out[l], tm, fc)
    return x2.reshape(batch, seq, d)
```

```python
import functools
import math

import jax
import jax.numpy as jnp
from jax import lax
from jax.experimental import pallas as pl
from jax.experimental.pallas import tpu as pltpu

F32 = jnp.float32
BF16 = jnp.bfloat16
I32 = jnp.int32
I16 = jnp.int16

HEAD_DIM = 64
HALF = HEAD_DIM // 2
A_HEADS = 8
IDX_HEADS = 4
B_HEADS = 4
TOPK_MAX = 256
ROPE_THETA = 10000.0
EPS = 1e-6
LANES = 128
SUBLANES = 8
MASKED = -1e30
INT_MIN = -(2 ** 31)
LOG2E = math.log2(math.e)
VMEM_LIMIT = 56 * 1024 * 1024
ONES_ROWS = 16
AHEAD = 6

D_A = A_HEADS * HEAD_DIM
D_B = B_HEADS * 2 * HEAD_DIM
IDX_W = IDX_HEADS * HEAD_DIM

SEG_AQ = 0
SEG_AK = SEG_AQ + D_A
SEG_AV = SEG_AK + D_A
SEG_IQ = SEG_AV + D_A
SEG_IK = SEG_IQ + IDX_W
SEG_IW = SEG_IK + LANES
SEG_BQ = SEG_IW + LANES
SEG_BK = SEG_BQ + D_B
SEG_BV = SEG_BK + D_B
SEG_G = SEG_BV + D_B


def _const_spec(shape):
    nd = len(shape)
    return pl.BlockSpec(shape, lambda *_: (0,) * nd, pipeline_mode=pl.Buffered(1))


def _params(*sem):
    return pltpu.CompilerParams(dimension_semantics=sem, vmem_limit_bytes=VMEM_LIMIT)


def _trig_kernel(pos_ref, invf_ref, sign_ref, cos_ref, sin_ref):
    ang = pos_ref[...].astype(F32) * invf_ref[...]
    cos_ref[...] = jnp.cos(ang)
    sin_ref[...] = jnp.sin(ang) * sign_ref[...]


def _rope_tables(positions):
    n = positions.size
    inv_freq = 1.0 / (ROPE_THETA ** (jnp.arange(0, HEAD_DIM, 2, dtype=F32) / HEAD_DIM))
    invf = jnp.tile(inv_freq, LANES // HALF)[None, :]
    sign = jnp.tile(jnp.concatenate([-jnp.ones(HALF, F32), jnp.ones(HALF, F32)]),
                    LANES // HEAD_DIM)[None, :]
    tr = min(n, 2048)
    return pl.pallas_call(
        _trig_kernel,
        grid=(n // tr,),
        in_specs=[pl.BlockSpec((tr, 1), lambda i: (i, 0)), _const_spec((1, LANES)),
                  _const_spec((1, LANES))],
        out_specs=[pl.BlockSpec((tr, LANES), lambda i: (i, 0))] * 2,
        out_shape=[jax.ShapeDtypeStruct((n, LANES), F32)] * 2,
        compiler_params=_params("parallel"),
        name="rope_trig",
    )(positions.reshape(n, 1), invf, sign)


def _rope(y, cos, sin, first_half):
    cols = []
    for c in range(y.shape[1] // LANES):
        t = y[:, c * LANES:(c + 1) * LANES]
        partner = jnp.where(first_half, pltpu.roll(t, LANES - HALF, 1), pltpu.roll(t, HALF, 1))
        cols.append(t * cos + partner * sin)
    return cols[0] if len(cols) == 1 else jnp.concatenate(cols, axis=1)


def _head_sumsq(y, bd):
    sq = (y * y).astype(BF16)
    w = min(bd.shape[0], y.shape[1])
    bd = bd[:w, :w]
    cols = [jnp.dot(sq[:, c * w:(c + 1) * w], bd, preferred_element_type=F32)
            for c in range(y.shape[1] // w)]
    return cols[0] if len(cols) == 1 else jnp.concatenate(cols, axis=1)


def _inproj_kernel(x_ref, gn_ref, w_ref, bd_ref, gaq_ref, gak_ref, gik_ref, gbq_ref, gbk_ref,
                   cos_ref, sin_ref,
                   aq_ref, ak_ref, avt_ref, iq_ref, ik_ref, iw_ref, bq_ref, bk_ref, bvt_ref, g_ref):
    x = x_ref[...]
    ms = jnp.mean(x * x, axis=-1, keepdims=True)
    h = (x * lax.rsqrt(ms + EPS) * gn_ref[...]).astype(BF16)
    cos = cos_ref[...]
    sin = sin_ref[...]
    bd = bd_ref[...]
    lane = lax.broadcasted_iota(I32, (1, LANES), 1)
    first_half = (lane % HEAD_DIM) < HALF
    scale = HEAD_DIM ** -0.5

    def proj(start, width):
        return jnp.dot(h, w_ref[:, start:start + width], preferred_element_type=F32)

    def store_transposed(out_ref, y):
        tq = out_ref.shape[2]
        for s in range(out_ref.shape[0]):
            out_ref[s] = y[s * tq:(s + 1) * tq, :].T.astype(BF16)

    def normed_rope(start, width, gain_ref, out_scale):
        y = proj(start, width)
        y = y * lax.rsqrt(_head_sumsq(y, bd) * (1.0 / HEAD_DIM) + EPS) * gain_ref[...]
        y = _rope(y, cos, sin, first_half)
        return y if out_scale is None else y * out_scale

    aq_ref[...] = normed_rope(SEG_AQ, D_A, gaq_ref, scale * LOG2E).astype(BF16)
    ak_ref[...] = normed_rope(SEG_AK, D_A, gak_ref, None).astype(BF16)
    store_transposed(avt_ref, proj(SEG_AV, D_A))
    iq_ref[...] = _rope(proj(SEG_IQ, IDX_W), cos, sin, first_half) * scale
    ik_ref[...] = normed_rope(SEG_IK, LANES, gik_ref, None)
    iw_ref[...] = proj(SEG_IW, LANES) * (IDX_HEADS ** -0.5)
    bq_ref[...] = normed_rope(SEG_BQ, D_B, gbq_ref, scale * LOG2E).astype(BF16)
    bk_ref[...] = normed_rope(SEG_BK, D_B, gbk_ref, None).astype(BF16)
    store_transposed(bvt_ref, proj(SEG_BV, D_B))
    g_ref[...] = proj(SEG_G, g_ref.shape[1]).astype(BF16)


def _pack_kernel(w_ref, o_ref, *, d_model):
    widths = (D_A, D_A, D_A, IDX_W, HEAD_DIM, IDX_HEADS, D_B, D_B, D_B, 2 * d_model)
    src = [0]
    for wd in widths:
        src.append(src[-1] + wd)
    dst = (SEG_AQ, SEG_AK, SEG_AV, SEG_IQ, None, None, SEG_BQ, SEG_BK, SEG_BV, SEG_G)
    for i, start in enumerate(dst):
        if start is not None:
            o_ref[0, :, start:start + widths[i]] = w_ref[0, :, src[i]:src[i + 1]].astype(BF16)
    ik = w_ref[0, :, src[4]:src[5]].astype(BF16)
    o_ref[0, :, SEG_IK:SEG_IK + HEAD_DIM] = ik
    o_ref[0, :, SEG_IK + HEAD_DIM:SEG_IK + LANES] = ik
    rows = o_ref.shape[1]
    o_ref[0, :, SEG_IW:SEG_IW + LANES] = jnp.zeros((rows, LANES), BF16)
    o_ref[0, :, SEG_IW:SEG_IW + IDX_HEADS] = w_ref[0, :, src[5]:src[6]].astype(BF16)


def _pack_w_in(w_in):
    depth, d, d_in = w_in.shape
    dp = SEG_G + 2 * d
    rows = 128 if d % 128 == 0 else d
    return pl.pallas_call(
        functools.partial(_pack_kernel, d_model=d),
        grid=(depth, d // rows),
        in_specs=[pl.BlockSpec((1, rows, d_in), lambda l, i: (l, i, 0))],
        out_specs=pl.BlockSpec((1, rows, dp), lambda l, i: (l, i, 0)),
        out_shape=jax.ShapeDtypeStruct((depth, d, dp), BF16),
        compiler_params=_params("parallel", "parallel"),
        name="pack_w_in",
    )(w_in)


def _in_projection(x2, cos, sin, attn_norm, wp, a_q_norm, a_k_norm, idx_k_norm, b_q_norm,
                   b_k_norm, tm, tq):
    n, d = x2.shape
    dp = wp.shape[1]
    blk = 2 * LANES
    r = jnp.arange(blk)
    bd = (r[:, None] // HEAD_DIM == r[None, :] // HEAD_DIM).astype(BF16)

    def tiled(g, reps):
        return jnp.tile(g.astype(F32), reps)[None, :]

    row = lambda w: pl.BlockSpec((tm, w), lambda i: (i, 0))
    out_w = [(D_A, BF16), (D_A, BF16), None, (IDX_W, F32), (LANES, F32), (LANES, F32),
             (D_B, BF16), (D_B, BF16), None, (2 * d, BF16)]
    vt_spec = pl.BlockSpec((tm // tq, D_A, tq), lambda i: (i, 0, 0))
    vt_shape = jax.ShapeDtypeStruct((n // tq, D_A, tq), BF16)
    return pl.pallas_call(
        _inproj_kernel,
        grid=(n // tm,),
        in_specs=[row(d), _const_spec((1, d)), _const_spec((d, dp)), _const_spec((blk, blk)),
                  _const_spec((1, D_A)), _const_spec((1, D_A)), _const_spec((1, LANES)),
                  _const_spec((1, D_B)), _const_spec((1, D_B)), row(LANES), row(LANES)],
        out_specs=[vt_spec if o is None else row(o[0]) for o in out_w],
        out_shape=[vt_shape if o is None else jax.ShapeDtypeStruct((n, o[0]), o[1]) for o in out_w],
        compiler_params=_params("parallel"),
        name="in_projection",
    )(x2, attn_norm.astype(F32)[None, :], wp, bd,
      tiled(a_q_norm, A_HEADS), tiled(a_k_norm, A_HEADS), tiled(idx_k_norm, 2),
      tiled(b_q_norm, 2 * B_HEADS), tiled(b_k_norm, 2 * B_HEADS), cos, sin)


def _dot_t(a, b):
    return lax.dot_general(a, b, (((1,), (1,)), ((), ())), preferred_element_type=F32)


def _logits(q, k, bias):
    s = _dot_t(k, q)
    return s if bias is None else s + bias


def _softmax_update(state, s, vt):
    m, acc = state
    m_new = jnp.maximum(m, jnp.max(s, axis=0, keepdims=True))
    alpha = jnp.exp2(m - m_new)
    p = jnp.exp2(s - m_new).astype(BF16)
    vt1 = jnp.concatenate([vt, jnp.ones((ONES_ROWS, vt.shape[1]), BF16)], axis=0)
    acc = alpha * acc + jnp.dot(vt1, p, preferred_element_type=F32)
    return m_new, acc


def _softmax_result(state, features):
    _, acc = state
    return acc[:features] / acc[features:features + 1]


def _advance_chains(states, steps):
    states = list(states)
    n = len(steps)
    logits = [None] * n
    for i in range(min(AHEAD, n)):
        logits[i] = steps[i][1]()
    for i in range(n):
        if i + AHEAD < n:
            logits[i + AHEAD] = steps[i + AHEAD][1]()
        chain, _, values_fn = steps[i]
        states[chain] = _softmax_update(states[chain], logits[i], values_fn())
        logits[i] = None
    return tuple(states)


def _stage_head_queries(q_ref, qm_scr, low_half):
    for hh in range(qm_scr.shape[0]):
        q2 = q_ref[:, (hh // 2) * LANES:(hh // 2 + 1) * LANES]
        own = low_half if hh % 2 == 0 else jnp.logical_not(low_half)
        qm_scr[hh] = jnp.where(own, q2, jnp.zeros_like(q2))


def _softmax_init(features, tq):
    return jnp.full((1, tq), MASKED, F32), jnp.zeros((features + ONES_ROWS, tq), F32)


def _sparse_kernel(iq_ref, iw_ref, ik_ref, aq_ref, ak_ref, avt_ref, o_ref,
                   key_scr, tie_scr, bias_scr, qm_scr, hi_scr, lo_scr, *, k_sel, tq):
    j = pl.program_id(1)
    n_chunks = j + 1
    nq = key_scr.shape[0]
    lane = lax.broadcasted_iota(I32, (1, LANES), 1)
    low_half = lane < HEAD_DIM
    q_pos = j * tq + lax.broadcasted_iota(I32, (1, tq), 1)
    key_iota = lax.broadcasted_iota(I32, (tq, 1), 0)

    w_t = iw_ref[...].T
    iq_heads = []
    for hh in range(IDX_HEADS):
        pair = iq_ref[:, (hh // 2) * LANES:(hh // 2 + 1) * LANES]
        own = low_half if hh % 2 == 0 else jnp.logical_not(low_half)
        iq_heads.append(jnp.where(own, pair, 0.0).astype(BF16))

    def keys_of(c, dots):
        score = w_t[0:1, :] * jnp.maximum(dots[0], 0.0)
        for hh in range(1, IDX_HEADS):
            score = score + w_t[hh:hh + 1, :] * jnp.maximum(dots[hh], 0.0)
        score = jnp.where(score == 0.0, 0.0, score)
        score = jnp.where(c * tq + key_iota <= q_pos, score, -jnp.inf)
        bits = lax.bitcast_convert_type(score, I32)
        key = bits ^ (lax.shift_right_arithmetic(bits, 31) & jnp.int32(0x7FFFFFFF))
        key_scr[c] = key
        hi_scr[c] = lax.shift_right_arithmetic(key, 16).astype(I16)

    def head_dots(c):
        ik = ik_ref[pl.ds(pl.multiple_of(c * tq, tq), tq), :].astype(BF16)
        return [_dot_t(ik, iq_heads[hh]) for hh in range(IDX_HEADS)]

    def score_pair(c2, carry):
        first, second = head_dots(2 * c2), head_dots(2 * c2 + 1)
        keys_of(2 * c2, first)
        keys_of(2 * c2 + 1, second)
        return carry

    lax.fori_loop(0, (j + 2) // 2, score_pair, 0)

    def count(scr, pred):
        rows_per_vreg = SUBLANES * 4 // scr.dtype.itemsize
        acc_dtype = F32 if scr.dtype.itemsize == 4 else I16
        one = jnp.ones((rows_per_vreg, tq), acc_dtype)
        zero = jnp.zeros((rows_per_vreg, tq), acc_dtype)

        def body(c, acc):
            parts = [acc]
            for r in range(tq // rows_per_vreg):
                rows = scr[c, r * rows_per_vreg:(r + 1) * rows_per_vreg, :]
                parts.append(jnp.where(pred(rows), one, zero))
            while len(parts) > 1:
                parts = [parts[i] + parts[i + 1] if i + 1 < len(parts) else parts[i]
                         for i in range(0, len(parts), 2)]
            return parts[0]
        acc = lax.fori_loop(0, n_chunks, body, zero)
        return jnp.sum(acc.astype(F32), axis=0, keepdims=True)

    def accept(carry, cand_u, cnt):
        t_u, c_ge = carry
        ok = cnt >= k_sel
        return jnp.where(ok, cand_u, t_u), jnp.where(ok, cnt, c_ge)

    def hi_bit(i, carry):
        cand_u = carry[0] | lax.shift_left(jnp.int32(1), 31 - i)
        cand_hi = lax.shift_right_arithmetic(cand_u ^ jnp.int32(INT_MIN), 16).astype(I16)
        return accept(carry, cand_u, count(hi_scr, lambda hh: hh >= cand_hi))

    all_keys = jnp.zeros((1, tq), F32) + (n_chunks * tq).astype(F32)
    carry = lax.fori_loop(0, 16, hi_bit, (jnp.zeros((1, tq), I32), all_keys))

    hi_sel = lax.shift_right_arithmetic(carry[0] ^ jnp.int32(INT_MIN), 16)
    hi_sel16 = hi_sel.astype(I16)
    c_above = count(hi_scr, lambda hh: hh > hi_sel16)
    low_min = -(2 ** 15)

    def lo_chunk(c, carry):
        key = key_scr[c]
        low = (key & jnp.int32(0xFFFF)) + low_min
        same_hi = lax.shift_right_arithmetic(key, 16) == hi_sel
        lo_scr[c] = jnp.where(same_hi, low, low_min).astype(I16)
        return carry

    lax.fori_loop(0, n_chunks, lo_chunk, 0)

    def lo_bit(i, carry):
        cand_u = carry[0] | lax.shift_left(jnp.int32(1), 31 - i)
        cand_lo = ((cand_u & jnp.int32(0xFFFF)) + low_min).astype(I16)
        return accept(carry, cand_u, c_above + count(lo_scr, lambda ll: ll >= cand_lo))

    t_u, c_ge = lax.fori_loop(16, 32, lo_bit, carry)
    tau = t_u ^ jnp.int32(INT_MIN)
    n_keys = nq * tq
    tie_break = jnp.max(c_ge) > k_sel
    bias_scr[jnp.minimum(j + 1, nq - 1)] = jnp.full((tq, tq), MASKED, F32)

    @pl.when(jnp.logical_not(tie_break))
    def _():
        def bias_chunk(c, carry):
            sel = (key_scr[c] >= tau) & (c * tq + key_iota <= q_pos)
            bias_scr[c] = jnp.where(sel, 0.0, MASKED)
            return carry
        lax.fori_loop(0, n_chunks, bias_chunk, 0)

    @pl.when(tie_break)
    def _():
        need = k_sel - count(key_scr, lambda kk: kk > tau)

        def tie_chunk(c, carry):
            tie_scr[c] = jnp.where(key_scr[c] == tau, c * tq + key_iota, n_keys)
            return carry
        lax.fori_loop(0, n_chunks, tie_chunk, 0)
        idx_bits = max(1, (n_keys - 1).bit_length())

        def idx_bit(i, p):
            cand = p | lax.shift_left(jnp.int32(1), idx_bits - 1 - i)
            cnt = count(tie_scr, lambda tt: tt < cand)
            return jnp.where(cnt < need, cand, p)
        p_tie = lax.fori_loop(0, idx_bits, idx_bit, jnp.zeros((1, tq), I32))

        def bias_chunk(c, carry):
            sel = (key_scr[c] > tau) | (tie_scr[c] <= p_tie)
            sel = sel & (c * tq + key_iota <= q_pos)
            bias_scr[c] = jnp.where(sel, 0.0, MASKED)
            return carry
        lax.fori_loop(0, n_chunks, bias_chunk, 0)

    def chunk_steps(c):
        start = pl.multiple_of(c * tq, tq)
        steps = []
        for pr in range(A_HEADS // 2):
            cols = slice(pr * LANES, (pr + 1) * LANES)
            for half in range(2):
                hh = 2 * pr + half

                def chain_logits(cols=cols, hh=hh):
                    return _logits(qm_scr[hh], ak_ref[pl.ds(start, tq), cols], bias_scr[c])

                def chain_values(hh=hh):
                    return avt_ref[c, hh * HEAD_DIM:(hh + 1) * HEAD_DIM, :]
                steps.append((hh, chain_logits, chain_values))
        return steps

    def attend(c2, states):
        return _advance_chains(states, chunk_steps(2 * c2) + chunk_steps(2 * c2 + 1))

    _stage_head_queries(aq_ref, qm_scr, low_half)
    states = lax.fori_loop(0, (j + 2) // 2, attend,
                           tuple(_softmax_init(HEAD_DIM, tq) for _ in range(A_HEADS)))
    for pr in range(A_HEADS // 2):
        out_t = jnp.concatenate([_softmax_result(states[2 * pr], HEAD_DIM),
                                 _softmax_result(states[2 * pr + 1], HEAD_DIM)], axis=0)
        o_ref[:, pr * LANES:(pr + 1) * LANES] = out_t.T.astype(BF16)


def _sparse_attention(iq, iw, ik, aq, ak, avt, batch, seq, tq):
    k_sel = min(TOPK_MAX, seq // 4)
    nq = seq // tq
    qrow = lambda w: pl.BlockSpec((tq, w), lambda b, j: (b * nq + j, 0))
    full = lambda w: pl.BlockSpec((seq, w), lambda b, j: (b, 0))
    return pl.pallas_call(
        functools.partial(_sparse_kernel, k_sel=k_sel, tq=tq),
        grid=(batch, nq),
        in_specs=[qrow(IDX_W), qrow(LANES), full(LANES), qrow(D_A), full(D_A),
                  pl.BlockSpec((nq, D_A, tq), lambda b, j: (b, 0, 0))],
        out_specs=qrow(D_A),
        out_shape=jax.ShapeDtypeStruct((batch * seq, D_A), BF16),
        scratch_shapes=[pltpu.VMEM((nq, tq, tq), I32), pltpu.VMEM((nq, tq, tq), I32),
                        pltpu.VMEM((nq, tq, tq), F32), pltpu.VMEM((A_HEADS, tq, LANES), BF16),
                        pltpu.VMEM((nq, tq, tq), I16), pltpu.VMEM((nq, tq, tq), I16)],
        compiler_params=_params("parallel", "arbitrary"),
        name="sparse_mixer",
    )(iq, iw, ik, aq, ak, avt)


def _diff_kernel(dl_ref, sub_ref, bq_ref, bk_ref, bvt_ref, o_ref, qm_scr, *, tq, lam_init):
    j = pl.program_id(1)
    lane = lax.broadcasted_iota(I32, (1, LANES), 1)
    low_half = lane < HEAD_DIM
    vdim = 2 * HEAD_DIM
    q_pos = j * tq + lax.broadcasted_iota(I32, (1, tq), 1)
    key_iota = lax.broadcasted_iota(I32, (tq, 1), 0)

    def chunk_steps(c, causal):
        start = pl.multiple_of(c * tq, tq)
        bias = jnp.where(c * tq + key_iota <= q_pos, 0.0, MASKED) if causal else None
        steps = []
        for hh in range(B_HEADS):
            cols = slice(hh * LANES, (hh + 1) * LANES)
            for cc in range(2):
                def chain_logits(cols=cols, ch=2 * hh + cc):
                    return _logits(qm_scr[ch], bk_ref[pl.ds(start, tq), cols], bias)

                def chain_values(hh=hh):
                    return bvt_ref[c, hh * vdim:(hh + 1) * vdim, :]
                steps.append((2 * hh + cc, chain_logits, chain_values))
        return steps

    def pair(c2, states, causal):
        return _advance_chains(states,
                               chunk_steps(2 * c2, causal) + chunk_steps(2 * c2 + 1, causal))

    last = (j + 2) // 2 - 1
    _stage_head_queries(bq_ref, qm_scr, low_half)
    init = tuple(_softmax_init(vdim, tq) for _ in range(2 * B_HEADS))
    states = lax.fori_loop(0, last, lambda c2, st: pair(c2, st, False), init)
    states = pair(last, states, True)

    dl = dl_ref[...]
    lam = (jnp.exp(jnp.sum(dl[0:1] * dl[1:2], axis=1, keepdims=True))
           - jnp.exp(jnp.sum(dl[2:3] * dl[3:4], axis=1, keepdims=True)) + lam_init)
    for hh in range(B_HEADS):
        y = (_softmax_result(states[2 * hh], vdim)
             - lam * _softmax_result(states[2 * hh + 1], vdim))
        ms = jnp.mean(y * y, axis=0, keepdims=True)
        y = y * lax.rsqrt(ms + EPS) * sub_ref[...] * (1.0 - lam_init)
        o_ref[:, hh * LANES:(hh + 1) * LANES] = y.T.astype(BF16)


def _diff_attention(diff_lambda, b_subln, bq, bk, bvt, batch, seq, tq, layer_idx):
    nq = seq // tq
    lam_init = 0.8 - 0.6 * math.exp(-0.3 * layer_idx)
    subln = jnp.broadcast_to(b_subln.astype(F32)[:, None], (2 * HEAD_DIM, tq))
    qrow = lambda w: pl.BlockSpec((tq, w), lambda b, j: (b * nq + j, 0))
    full = lambda w: pl.BlockSpec((seq, w), lambda b, j: (b, 0))
    return pl.pallas_call(
        functools.partial(_diff_kernel, tq=tq, lam_init=lam_init),
        grid=(batch, nq),
        in_specs=[_const_spec((4, HEAD_DIM)), _const_spec((2 * HEAD_DIM, tq)), qrow(D_B),
                  full(D_B), pl.BlockSpec((nq, D_B, tq), lambda b, j: (b, 0, 0))],
        out_specs=qrow(D_B),
        out_shape=jax.ShapeDtypeStruct((batch * seq, D_B), BF16),
        scratch_shapes=[pltpu.VMEM((2 * B_HEADS, tq, LANES), BF16)],
        compiler_params=_params("parallel", "arbitrary"),
        name="diff_mixer",
    )(diff_lambda.astype(F32), subln, bq, bk, bvt)


def _sigmoid(z):
    return 1.0 / (1.0 + jnp.exp(-z))


def _out_ffn_kernel(x_ref, ya_ref, yb_ref, g_ref, gb_ref, wua_ref, wub_ref, wo_ref, fn_ref,
                    wg_ref, wu_ref, wd_ref, o_ref, *, fc):
    d = x_ref.shape[1]
    ua = jnp.dot(ya_ref[...], wua_ref[...], preferred_element_type=F32)
    ub = jnp.dot(yb_ref[...], wub_ref[...], preferred_element_type=F32)
    g = _sigmoid(g_ref[...].astype(F32) + gb_ref[...])
    merged = g[:, :d] * ua + g[:, d:] * ub
    x1 = x_ref[...] + jnp.dot(merged.astype(BF16), wo_ref[...], preferred_element_type=F32)
    ms = jnp.mean(x1 * x1, axis=-1, keepdims=True)
    h = (x1 * lax.rsqrt(ms + EPS) * fn_ref[...]).astype(BF16)
    acc = x1
    for c in range(wd_ref.shape[0] // fc):
        cols = slice(c * fc, (c + 1) * fc)
        gate = jnp.dot(h, wg_ref[:, cols], preferred_element_type=F32)
        up = jnp.dot(h, wu_ref[:, cols], preferred_element_type=F32)
        act = (gate * _sigmoid(gate) * up).astype(BF16)
        acc = acc + jnp.dot(act, wd_ref[cols, :], preferred_element_type=F32)
    o_ref[...] = acc


def _out_ffn(x2, ya, yb, gates, gate_bias, w_up_a, w_up_b, w_out, ffn_norm, w_ffn_in, w_ffn_out,
             tm, fc):
    n, d = x2.shape
    hidden = w_ffn_out.shape[0]
    wg = w_ffn_in[:, :hidden].astype(BF16)
    wu = w_ffn_in[:, hidden:].astype(BF16)
    row = lambda w: pl.BlockSpec((tm, w), lambda i: (i, 0))
    return pl.pallas_call(
        functools.partial(_out_ffn_kernel, fc=fc),
        grid=(n // tm,),
        in_specs=[row(d), row(D_A), row(D_B), row(2 * d), _const_spec((1, 2 * d)),
                  _const_spec((D_A, d)), _const_spec((D_B, d)), _const_spec((d, d)),
                  _const_spec((1, d)), _const_spec((d, hidden)), _const_spec((d, hidden)),
                  _const_spec((hidden, d))],
        out_specs=row(d),
        out_shape=jax.ShapeDtypeStruct((n, d), F32),
        compiler_params=_params("parallel"),
        name="out_ffn",
    )(x2, ya, yb, gates, gate_bias.astype(F32)[None, :], w_up_a.astype(BF16),
      w_up_b.astype(BF16), w_out.astype(BF16), ffn_norm.astype(F32)[None, :], wg, wu,
      w_ffn_out.astype(BF16))


def _tiles(batch, seq, hidden):
    n = batch * seq
    tm = 512 if n % 512 == 0 else seq
    tq = 256
    assert seq % (2 * tq) == 0 and n % tm == 0 and tm % tq == 0
    fc = 256 if hidden % 256 == 0 else hidden
    return tm, tq, fc


def kernel(x, positions, attn_norm, w_in, gate_bias, a_q_norm, a_k_norm, idx_k_norm, b_q_norm, b_k_norm, diff_lambda, b_subln, w_up_a, w_up_b, w_out, ffn_norm, w_ffn_in, w_ffn_out):
    batch, seq, d = x.shape
    depth = w_in.shape[0]
    tm, tq, fc = _tiles(batch, seq, w_ffn_out.shape[1])
    cos, sin = _rope_tables(positions)
    w_in_packed = _pack_w_in(w_in)
    x2 = x.reshape(batch * seq, d)
    for l in range(depth):
        aq, ak, avt, iq, ik, iw, bq, bk, bvt, gates = _in_projection(
            x2, cos, sin, attn_norm[l], w_in_packed[l], a_q_norm[l], a_k_norm[l], idx_k_norm[l],
            b_q_norm[l], b_k_norm[l], tm, tq)
        ya = _sparse_attention(iq, iw, ik, aq, ak, avt, batch, seq, tq)
        yb = _diff_attention(diff_lambda[l], b_subln[l], bq, bk, bvt, batch, seq, tq, l)
        x2 = _out_ffn(x2, ya, yb, gates, gate_bias[l], w_up_a[l], w_up_b[l], w_out[l],
                      ffn_norm[l], w_ffn_in[l], w_ffn_out[l], tm, fc)
    return x2.reshape(batch, seq, d)
```
